```python
import math
import jax, jax.numpy as jnp
from jax import lax
import numpy as np

D_MODEL = 2048
BATCH = 2
SEQ = 4096
DEPTH = 1

MEM_LEN = 256
CONV_WIDTH = D_MODEL // 2
CONV_GROUPS = 8
CONV_KERNEL = 31
DIFF_HEADS = 8
DIFF_HEAD_DIM = (D_MODEL - CONV_WIDTH) // DIFF_HEADS // 2
DIFF_V_DIM = 2 * DIFF_HEAD_DIM
ATTN_QK_WIDTH = DIFF_HEADS * 2 * DIFF_HEAD_DIM
ATTN_V_WIDTH = DIFF_HEADS * DIFF_V_DIM
MIX_WIDTH = CONV_WIDTH + ATTN_V_WIDTH
IN_WIDTH = 2 * CONV_WIDTH + 2 * ATTN_QK_WIDTH + ATTN_V_WIDTH
Q_BLOCK = 128
CROSS_HEADS = 4
CROSS_HEAD_DIM = D_MODEL // CROSS_HEADS
PEER_HEADS = 8
PEER_N_KEYS = 128
PEER_N_EXPERTS = PEER_N_KEYS * PEER_N_KEYS
PEER_TOPK = 16
PEER_KEY_DIM = 256
PEER_HALF = PEER_KEY_DIM // 2
PEER_CHUNK = 128
RMS_EPS = 1e-6
LN_EPS = 1e-5
NEG_INF = -1e30

kernel_name = "hybrid_conv_diffattn_peer_block"


def rmsnorm(x, g):
    xf = x.astype(jnp.float32)
    y = xf * lax.rsqrt(jnp.mean(xf * xf, axis=-1, keepdims=True) + RMS_EPS)
    return (y * g.astype(jnp.float32)).astype(x.dtype)


def lambda_init_for(layer_idx):
    return 0.8 - 0.6 * math.exp(-0.3 * layer_idx)


def conformer_conv_group(z, w_dw, b_dw, ln_g, ln_b):
    B, S, _ = z.shape
    a, gate = jnp.split(z, 2, axis=-1)
    u = a * jax.nn.sigmoid(gate)
    y = lax.conv_general_dilated(
        u, w_dw[:, None, :].astype(u.dtype), window_strides=(1,),
        padding=[(CONV_KERNEL - 1, 0)],
        dimension_numbers=('NWC', 'WIO', 'NWC'),
        feature_group_count=CONV_WIDTH) + b_dw
    yf = y.astype(jnp.float32).reshape(B, S, CONV_GROUPS, CONV_WIDTH // CONV_GROUPS)
    mu = jnp.mean(yf, axis=-1, keepdims=True)
    var = jnp.mean(jnp.square(yf - mu), axis=-1, keepdims=True)
    yf = ((yf - mu) * lax.rsqrt(var + LN_EPS)).reshape(B, S, CONV_WIDTH)
    yf = yf * ln_g.astype(jnp.float32) + ln_b.astype(jnp.float32)
    return jax.nn.silu(yf).astype(z.dtype)


def diff_attention_group(q, k, v, lam, subln_g, lam_init):
    B, S = q.shape[0], q.shape[1]
    nb = S // Q_BLOCK
    scale = DIFF_HEAD_DIM ** -0.5
    k_pos = jnp.arange(S)

    def block(i):
        qb = lax.dynamic_slice_in_dim(q, i * Q_BLOCK, Q_BLOCK, axis=1)
        s = jnp.einsum('bqhcd,bkhcd->bchqk', qb, k).astype(jnp.float32) * scale
        q_pos = i * Q_BLOCK + jnp.arange(Q_BLOCK)
        mask = k_pos[None, :] <= q_pos[:, None]
        p = jax.nn.softmax(jnp.where(mask, s, NEG_INF), axis=-1)
        a = p[:, 0] - lam * p[:, 1]
        o = jnp.einsum('bhqk,bkhd->bqhd', a.astype(v.dtype), v)
        of = o.astype(jnp.float32)
        of = of * lax.rsqrt(jnp.mean(of * of, axis=-1, keepdims=True) + RMS_EPS)
        of = of * subln_g.astype(jnp.float32) * (1.0 - lam_init)
        return of.astype(v.dtype)

    out = lax.map(block, jnp.arange(nb))
    return out.transpose(1, 0, 2, 3, 4).reshape(B, S, ATTN_V_WIDTH)


def memory_cross_attention(hn, mem, mem_g, w_cq, w_ckv, w_co):
    B, S, D = hn.shape
    mn = rmsnorm(mem, mem_g)
    q = (hn @ w_cq).reshape(B, S, CROSS_HEADS, CROSS_HEAD_DIM)
    kv = (mn @ w_ckv).reshape(B, mem.shape[1], 2, CROSS_HEADS, CROSS_HEAD_DIM)
    k, v = kv[:, :, 0], kv[:, :, 1]
    s = jnp.einsum('bqhd,bkhd->bhqk', q, k).astype(jnp.float32) * (CROSS_HEAD_DIM ** -0.5)
    p = jax.nn.softmax(s, axis=-1).astype(v.dtype)
    o = jnp.einsum('bhqk,bkhd->bqhd', p, v).reshape(B, S, D)
    return o @ w_co


def peer_ffn(xn, w_pq, keys1, keys2, u_tab, v_tab):
    B, S, D = xn.shape
    T = B * S
    xt = xn.reshape(T, D)
    q = (xt @ w_pq).reshape(T, PEER_HEADS, 2, PEER_HALF)
    s1 = jnp.einsum('thd,hnd->thn', q[:, :, 0], keys1).astype(jnp.float32)
    s2 = jnp.einsum('thd,hnd->thn', q[:, :, 1], keys2).astype(jnp.float32)
    v1, i1 = lax.top_k(s1, PEER_TOPK)
    v2, i2 = lax.top_k(s2, PEER_TOPK)
    cand = (v1[..., :, None] + v2[..., None, :]).reshape(T, PEER_HEADS, PEER_TOPK * PEER_TOPK)
    sc, ci = lax.top_k(cand, PEER_TOPK)
    e1 = jnp.take_along_axis(i1, ci // PEER_TOPK, axis=-1)
    e2 = jnp.take_along_axis(i2, ci % PEER_TOPK, axis=-1)
    ids = (e1 * PEER_N_KEYS + e2).reshape(T, PEER_HEADS * PEER_TOPK)
    g = jax.nn.softmax(sc, axis=-1).reshape(T, PEER_HEADS * PEER_TOPK)

    def chunk(args):
        xc, idc, gc = args
        uc = jnp.take(u_tab, idc, axis=0)
        h = jnp.einsum('cd,ckd->ck', xc, uc)
        w = gc.astype(xc.dtype) * jax.nn.gelu(h, approximate=False)
        return jnp.einsum('ck,ckd->cd', w, jnp.take(v_tab, idc, axis=0))

    nc = T // PEER_CHUNK
    y = lax.map(chunk, (xt.reshape(nc, PEER_CHUNK, D),
                        ids.reshape(nc, PEER_CHUNK, -1),
                        g.reshape(nc, PEER_CHUNK, -1)))
    return y.reshape(B, S, D)


def setup_inputs(seed: int = 0) -> dict:
    key = jax.random.key(seed)
    ks = iter(jax.random.split(key, 32))
    L, D = DEPTH, D_MODEL
    f32 = jnp.float32

    def nrm(shape, scale):
        return jax.random.normal(next(ks), shape, f32) * scale

    def gain(shape):
        return 1.0 + 0.02 * jax.random.normal(next(ks), shape, f32)

    return {
        "x": nrm((BATCH, SEQ, D), 1.0),
        "mem": nrm((BATCH, MEM_LEN, D), 1.0),
        "norm_mix_g": gain((L, D)),
        "w_in": nrm((L, D, IN_WIDTH), D ** -0.5),
        "conv_dw_w": nrm((L, CONV_KERNEL, CONV_WIDTH), CONV_KERNEL ** -0.5),
        "conv_dw_b": nrm((L, CONV_WIDTH), 0.02),
        "conv_ln_g": gain((L, CONV_WIDTH)),
        "conv_ln_b": nrm((L, CONV_WIDTH), 0.02),
        "lambda_q1": nrm((L, DIFF_HEAD_DIM), 0.1),
        "lambda_k1": nrm((L, DIFF_HEAD_DIM), 0.1),
        "lambda_q2": nrm((L, DIFF_HEAD_DIM), 0.1),
        "lambda_k2": nrm((L, DIFF_HEAD_DIM), 0.1),
        "diff_subln_g": gain((L, DIFF_V_DIM)),
        "w_out": nrm((L, MIX_WIDTH, D), MIX_WIDTH ** -0.5),
        "norm_cross_g": gain((L, D)),
        "norm_mem_g": gain((L, D)),
        "w_cq": nrm((L, D, D), D ** -0.5),
        "w_ckv": nrm((L, D, 2 * D), D ** -0.5),
        "w_co": nrm((L, D, D), D ** -0.5),
        "norm_peer_g": gain((L, D)),
        "w_pq": nrm((L, D, PEER_HEADS * PEER_KEY_DIM), D ** -0.5),
        "peer_keys1": nrm((L, PEER_HEADS, PEER_N_KEYS, PEER_HALF), PEER_HALF ** -0.5),
        "peer_keys2": nrm((L, PEER_HEADS, PEER_N_KEYS, PEER_HALF), PEER_HALF ** -0.5),
        "peer_u": nrm((L, PEER_N_EXPERTS, D), D ** -0.5),
        "peer_v": nrm((L, PEER_N_EXPERTS, D), PEER_HEADS ** -0.5),
        "final_norm_g": gain((D,)),
    }


def reference(x, mem, norm_mix_g, w_in, conv_dw_w, conv_dw_b, conv_ln_g, conv_ln_b,
              lambda_q1, lambda_k1, lambda_q2, lambda_k2, diff_subln_g, w_out,
              norm_cross_g, norm_mem_g, w_cq, w_ckv, w_co,
              norm_peer_g, w_pq, peer_keys1, peer_keys2, peer_u, peer_v, final_norm_g):
    B, S, D = x.shape
    h = x
    for l in range(DEPTH):
        hn = rmsnorm(h, norm_mix_g[l])
        z = hn @ w_in[l]
        o0 = 2 * CONV_WIDTH
        o1 = o0 + ATTN_QK_WIDTH
        o2 = o1 + ATTN_QK_WIDTH
        conv_out = conformer_conv_group(z[..., :o0], conv_dw_w[l], conv_dw_b[l],
                                        conv_ln_g[l], conv_ln_b[l])
        q = z[..., o0:o1].reshape(B, S, DIFF_HEADS, 2, DIFF_HEAD_DIM)
        k = z[..., o1:o2].reshape(B, S, DIFF_HEADS, 2, DIFF_HEAD_DIM)
        v = z[..., o2:].reshape(B, S, DIFF_HEADS, DIFF_V_DIM)
        lam_init = lambda_init_for(l)
        lam = (jnp.exp(jnp.sum(lambda_q1[l].astype(jnp.float32) * lambda_k1[l].astype(jnp.float32)))
               - jnp.exp(jnp.sum(lambda_q2[l].astype(jnp.float32) * lambda_k2[l].astype(jnp.float32)))
               + lam_init)
        attn_out = diff_attention_group(q, k, v, lam, diff_subln_g[l], lam_init)
        mix = jnp.concatenate([conv_out, attn_out], axis=-1)
        h = h + mix @ w_out[l]
        h = h + memory_cross_attention(rmsnorm(h, norm_cross_g[l]), mem, norm_mem_g[l],
                                       w_cq[l], w_ckv[l], w_co[l])
        h = h + peer_ffn(rmsnorm(h, norm_peer_g[l]), w_pq[l], peer_keys1[l], peer_keys2[l],
                         peer_u[l], peer_v[l])
    return rmsnorm(h, final_norm_g)
```

```python
import functools
import math

import jax
import jax.numpy as jnp
from jax import lax
from jax.experimental import pallas as pl
from jax.experimental.pallas import tpu as pltpu

F32 = jnp.float32
BF16 = jnp.bfloat16

RMS_EPS = 1e-6
LN_EPS = 1e-5
NEG_INF = -1e30

LANES = 128
CONV_KERNEL = 31
CONV_HALO = 32
DIFF_HEADS = 8
DIFF_HEAD_DIM = 64
CROSS_HEADS = 4
PEER_HEADS = 8
PEER_N_KEYS = 128
PEER_TOPK = 16
VMEM_LIMIT = 56 * 1024 * 1024

_NT = (((1,), (1,)), ((), ()))


def _params(*sem):
    return pltpu.CompilerParams(dimension_semantics=sem, vmem_limit_bytes=VMEM_LIMIT)


def _norm_mm_kernel(x_ref, g_ref, w_ref, o_ref, hn_ref):
    @pl.when(pl.program_id(1) == 0)
    def _():
        x = x_ref[...]
        ms = jnp.mean(x * x, axis=-1, keepdims=True)
        hn_ref[...] = (x * lax.rsqrt(ms + RMS_EPS) * g_ref[...]).astype(BF16)

    o_ref[...] = jnp.dot(hn_ref[...], w_ref[...],
                         preferred_element_type=F32).astype(o_ref.dtype)


def _norm_matmul(x, g, w, out_dtype, *, tm, tn, name):
    M, K = x.shape
    N = w.shape[1]
    return pl.pallas_call(
        _norm_mm_kernel,
        grid=(M // tm, N // tn),
        in_specs=[pl.BlockSpec((tm, K), lambda i, j: (i, 0)),
                  pl.BlockSpec((1, K), lambda i, j: (0, 0)),
                  pl.BlockSpec((K, tn), lambda i, j: (0, j))],
        out_specs=pl.BlockSpec((tm, tn), lambda i, j: (i, j)),
        out_shape=jax.ShapeDtypeStruct((M, N), out_dtype),
        scratch_shapes=[pltpu.VMEM((tm, K), BF16)],
        compiler_params=_params("parallel", "arbitrary"),
        name=name,
    )(x, g.reshape(1, K), w)


def _mm_res_kernel(*refs, n_pairs):
    res_ref, o_ref = refs[2 * n_pairs], refs[2 * n_pairs + 1]
    acc = res_ref[...]
    for p in range(n_pairs):
        acc = acc + jnp.dot(refs[2 * p][...], refs[2 * p + 1][...],
                            preferred_element_type=F32)
    o_ref[...] = acc


def _matmul_residual(pairs, res, *, tm, tn, name):
    M, N = res.shape
    in_specs, args = [], []
    for a, w in pairs:
        k = a.shape[1]
        in_specs += [pl.BlockSpec((tm, k), lambda i, j: (i, 0)),
                     pl.BlockSpec((k, tn), lambda i, j: (0, j))]
        args += [a, w]
    in_specs.append(pl.BlockSpec((tm, tn), lambda i, j: (i, j)))
    return pl.pallas_call(
        functools.partial(_mm_res_kernel, n_pairs=len(pairs)),
        grid=(M // tm, N // tn),
        in_specs=in_specs,
        out_specs=pl.BlockSpec((tm, tn), lambda i, j: (i, j)),
        out_shape=jax.ShapeDtypeStruct((M, N), F32),
        compiler_params=_params("parallel", "arbitrary"),
        name=name,
    )(*args, res)


def _conv_kernel(a_ref, gt_ref, ah_ref, gh_ref, w_ref, b_ref, lg_ref, lb_ref, o_ref, ubuf,
                 *, ts, rows):
    i = pl.program_id(2)
    uh = ah_ref[...] * jax.nn.sigmoid(gh_ref[...])
    ubuf[0:CONV_HALO, :] = jnp.where(i > 0, uh, 0.0)
    ubuf[CONV_HALO:, :] = a_ref[...] * jax.nn.sigmoid(gt_ref[...])
    bias = b_ref[...]
    lg = lg_ref[...]
    lb = lb_ref[...]
    first = CONV_HALO - (CONV_KERNEL - 1)
    for r in range(ts // rows):
        acc = jnp.zeros((rows, LANES), F32) + bias
        for j in range(CONV_KERNEL):
            start = r * rows + first + j
            acc = acc + w_ref[j:j + 1, :] * ubuf[start:start + rows, :]
        mu = jnp.mean(acc, axis=-1, keepdims=True)
        d = acc - mu
        var = jnp.mean(d * d, axis=-1, keepdims=True)
        y = d * lax.rsqrt(var + LN_EPS) * lg + lb
        o_ref[r * rows:(r + 1) * rows, :] = (y * jax.nn.sigmoid(y)).astype(o_ref.dtype)


def _conv_group(zc, w_dw, b_dw, ln_g, ln_b, *, batch, seq, ts=512, rows=128):
    T, C2 = zc.shape
    C = C2 // 2
    G = C // LANES
    ns = seq // ts
    hb = ts // CONV_HALO

    def main(off):
        return pl.BlockSpec((ts, LANES), lambda b, g, i: (b * ns + i, g + off))

    def halo(off):
        return pl.BlockSpec(
            (CONV_HALO, LANES),
            lambda b, g, i: (jnp.maximum((b * ns + i) * hb - 1, 0), g + off))

    vec = pl.BlockSpec((1, LANES), lambda b, g, i: (0, g))
    return pl.pallas_call(
        functools.partial(_conv_kernel, ts=ts, rows=rows),
        grid=(batch, G, ns),
        in_specs=[main(0), main(G), halo(0), halo(G),
                  pl.BlockSpec((CONV_KERNEL, LANES), lambda b, g, i: (0, g)),
                  vec, vec, vec],
        out_specs=pl.BlockSpec((ts, LANES), lambda b, g, i: (b * ns + i, g)),
        out_shape=jax.ShapeDtypeStruct((T, C), BF16),
        scratch_shapes=[pltpu.VMEM((ts + CONV_HALO, LANES), F32)],
        compiler_params=_params("parallel", "parallel", "arbitrary"),
        name="conv_group",
    )(zc, zc, zc, zc, w_dw, b_dw.reshape(1, C), ln_g.reshape(1, C), ln_b.reshape(1, C))


def _diff_attn_kernel(q_ref, k_ref, v_ref, lq1, lk1, lq2, lk2, sg_ref, o_ref,
                      m_ref, l_ref, acc_ref, *, tq, lam_init):
    qi = pl.program_id(2)
    q = q_ref[...]
    lane = lax.broadcasted_iota(jnp.int32, (tq, LANES), 1)
    zero = jnp.zeros_like(q)
    qq = jnp.concatenate([jnp.where(lane < DIFF_HEAD_DIM, q, zero),
                          jnp.where(lane >= DIFF_HEAD_DIM, q, zero)], axis=0)

    m_ref[...] = jnp.full(m_ref.shape, NEG_INF, F32)
    l_ref[...] = jnp.zeros(l_ref.shape, F32)
    acc_ref[...] = jnp.zeros(acc_ref.shape, F32)

    def step(ki, masked):
        start = pl.multiple_of(ki * tq, tq)
        kb = k_ref[pl.ds(start, tq), :]
        vb = v_ref[pl.ds(start, tq), :]
        s = lax.dot_general(qq, kb, _NT, preferred_element_type=F32)
        if masked:
            row = lax.broadcasted_iota(jnp.int32, (2 * tq, tq), 0)
            col = lax.broadcasted_iota(jnp.int32, (2 * tq, tq), 1)
            row = jnp.where(row >= tq, row - tq, row)
            s = jnp.where(col <= row, s, NEG_INF)
        m_old = m_ref[...]
        m_new = jnp.maximum(m_old, jnp.max(s, axis=-1, keepdims=True))
        alpha = jnp.exp(m_old - m_new)
        p = jnp.exp(s - m_new)
        l_ref[...] = alpha * l_ref[...] + jnp.sum(p, axis=-1, keepdims=True)
        acc_ref[...] = alpha * acc_ref[...] + jnp.dot(p.astype(BF16), vb,
                                                      preferred_element_type=F32)
        m_ref[...] = m_new

    def body(ki, c):
        step(ki, False)
        return c

    lax.fori_loop(0, qi, body, 0)
    step(qi, True)

    lam = (jnp.exp(jnp.sum(lq1[...] * lk1[...], axis=-1, keepdims=True))
           - jnp.exp(jnp.sum(lq2[...] * lk2[...], axis=-1, keepdims=True)) + lam_init)
    o = acc_ref[...] / l_ref[...]
    of = o[:tq] - lam * o[tq:]
    of = of * lax.rsqrt(jnp.mean(of * of, axis=-1, keepdims=True) + RMS_EPS)
    of = of * sg_ref[...] * (1.0 - lam_init)
    o_ref[...] = of.astype(o_ref.dtype)


def _diff_attention(zqkv, lq1, lk1, lq2, lk2, subln_g, *, batch, seq, lam_init, tq=512):
    T = zqkv.shape[0]
    H = DIFF_HEADS
    nq = seq // tq
    lam_spec = pl.BlockSpec((1, DIFF_HEAD_DIM), lambda b, h, i: (0, 0))
    return pl.pallas_call(
        functools.partial(_diff_attn_kernel, tq=tq, lam_init=lam_init),
        grid=(batch, H, nq),
        in_specs=[pl.BlockSpec((tq, LANES), lambda b, h, i: (b * nq + i, h)),
                  pl.BlockSpec((seq, LANES), lambda b, h, i: (b, H + h)),
                  pl.BlockSpec((seq, LANES), lambda b, h, i: (b, 2 * H + h)),
                  lam_spec, lam_spec, lam_spec, lam_spec,
                  pl.BlockSpec((1, LANES), lambda b, h, i: (0, 0))],
        out_specs=pl.BlockSpec((tq, LANES), lambda b, h, i: (b * nq + i, h)),
        out_shape=jax.ShapeDtypeStruct((T, H * LANES), BF16),
        scratch_shapes=[pltpu.VMEM((2 * tq, 1), F32),
                        pltpu.VMEM((2 * tq, 1), F32),
                        pltpu.VMEM((2 * tq, LANES), F32)],
        compiler_params=_params("parallel", "parallel", "arbitrary"),
        name="diff_attention",
    )(zqkv, zqkv, zqkv,
      lq1.reshape(1, -1), lk1.reshape(1, -1), lq2.reshape(1, -1), lk2.reshape(1, -1),
      subln_g.reshape(1, -1))


def _cross_attn_kernel(q_ref, k_ref, v_ref, o_ref, *, dh, scale):
    for h in range(CROSS_HEADS):
        sl = slice(h * dh, (h + 1) * dh)
        s = lax.dot_general(q_ref[:, sl], k_ref[:, sl], _NT,
                            preferred_element_type=F32) * scale
        p = jnp.exp(s - jnp.max(s, axis=-1, keepdims=True))
        l = jnp.sum(p, axis=-1, keepdims=True)
        o = jnp.dot(p.astype(BF16), v_ref[:, sl], preferred_element_type=F32)
        o_ref[:, sl] = (o / l).astype(o_ref.dtype)


def _cross_attention(q, kv, *, batch, seq, mem_len, tq=512):
    T, D = q.shape
    dh = D // CROSS_HEADS
    nq = seq // tq
    return pl.pallas_call(
        functools.partial(_cross_attn_kernel, dh=dh, scale=dh ** -0.5),
        grid=(batch, nq),
        in_specs=[pl.BlockSpec((tq, D), lambda b, i: (b * nq + i, 0)),
                  pl.BlockSpec((mem_len, D), lambda b, i: (b, 0)),
                  pl.BlockSpec((mem_len, D), lambda b, i: (b, 1))],
        out_specs=pl.BlockSpec((tq, D), lambda b, i: (b * nq + i, 0)),
        out_shape=jax.ShapeDtypeStruct((T, D), BF16),
        compiler_params=_params("parallel", "arbitrary"),
        name="cross_attention",
    )(q, kv, kv)


def _top_values(work, n):
    vals = []
    for _ in range(n):
        m = jnp.max(work, axis=0, keepdims=True)
        vals.append(m)
        work = jnp.where(work == m, -jnp.inf, work)
    return vals


def _peer_route_kernel(q_ref, k1_ref, k2_ref, thr_ref, e1_ref, s2_ref, e2_ref):
    K = PEER_TOPK
    q = q_ref[...]
    s1 = lax.dot_general(k1_ref[0], q[:, :PEER_N_KEYS], _NT, preferred_element_type=F32)
    s2 = lax.dot_general(k2_ref[0], q[:, PEER_N_KEYS:], _NT, preferred_element_type=F32)
    v1 = _top_values(s1, K + 1)
    v2 = _top_values(s2, K + 1)
    v1a, v1b = jnp.concatenate(v1[:8], axis=0), jnp.concatenate(v1[8:16], axis=0)
    v2a, v2b = jnp.concatenate(v2[:8], axis=0), jnp.concatenate(v2[8:16], axis=0)
    ninf = jnp.full_like(v1[0], -jnp.inf)
    cand = [v1[0] + v2a, v1[0] + v2b]
    cand += [v1[a] + v2a for a in range(1, 8)]
    cand += [v1b + v2[0]]
    cand += [jnp.concatenate([v1[K] + v2[0], v1[0] + v2[K]] + [ninf] * 6, axis=0)]
    top = _top_values(jnp.concatenate(cand, axis=0), K + 1)
    z = jnp.ones_like(top[0])
    for k in range(1, K):
        z = z + jnp.exp(top[k] - top[0])
    tau = 0.5 * (top[K - 1] + top[K])
    thr_ref[0] = tau - s1
    e1_ref[0] = jnp.exp(s1 - v1[0]) * (1.0 / z)
    s2_ref[0] = s2
    e2_ref[0] = jnp.exp(s2 - v2[0])


def _peer_route(q, keys1, keys2, *, tm=256):
    T = q.shape[0]
    H = PEER_HEADS
    kspec = pl.BlockSpec((1, PEER_N_KEYS, PEER_N_KEYS), lambda i, h: (h, 0, 0))
    ospec = pl.BlockSpec((1, PEER_N_KEYS, tm), lambda i, h: (h, 0, i))
    oshape = jax.ShapeDtypeStruct((H, PEER_N_KEYS, T), F32)
    return pl.pallas_call(
        _peer_route_kernel,
        grid=(T // tm, H),
        in_specs=[pl.BlockSpec((tm, 2 * PEER_N_KEYS), lambda i, h: (i, h)), kspec, kspec],
        out_specs=[ospec, ospec, ospec, ospec],
        out_shape=[oshape, oshape, oshape, oshape],
        compiler_params=_params("parallel", "arbitrary"),
        name="peer_route",
    )(q, keys1, keys2)


def _peer_dense_kernel(xn_ref, u_ref, vt_ref, thr_ref, e1_ref, s2_ref, e2_ref, res_ref, g_ref,
                       o_ref, yt_ref, ht_ref, wg_ref, *, tm, te):
    e = pl.program_id(1)

    @pl.when(e == 0)
    def _():
        yt_ref[...] = jnp.zeros(yt_ref.shape, F32)

    ht_ref[...] = lax.dot_general(u_ref[...], xn_ref[...], _NT, preferred_element_type=F32)

    for il in range(te // LANES):
        rs = slice(il * LANES, (il + 1) * LANES)
        for lt in range(tm // LANES):
            cs = slice(lt * LANES, (lt + 1) * LANES)
            gate = jnp.zeros((LANES, LANES), F32)
            for h in range(PEER_HEADS):
                thr = thr_ref[h, il:il + 1, cs]
                e1 = e1_ref[h, il:il + 1, cs]
                gate = gate + jnp.where(s2_ref[h, :, cs] >= thr, e2_ref[h, :, cs], 0.0) * e1
            hh = ht_ref[rs, cs]
            act = 0.5 * hh * (1.0 + lax.erf(hh * (2.0 ** -0.5)))
            wg_ref[rs, cs] = (gate * act).astype(BF16)
    yt_ref[...] += jnp.dot(vt_ref[...], wg_ref[...], preferred_element_type=F32)

    @pl.when(e == pl.num_programs(1) - 1)
    def _():
        h = res_ref[...] + yt_ref[...].T
        ms = jnp.mean(h * h, axis=-1, keepdims=True)
        o_ref[...] = h * lax.rsqrt(ms + RMS_EPS) * g_ref[...]


def _peer_dense(xn, u, vt, thr, e1, s2, e2, res, g, *, tm=512, te=1024):
    T, D = xn.shape
    E = u.shape[0]
    H = PEER_HEADS
    kt = te // LANES
    once = dict(pipeline_mode=pl.Buffered(1))
    key_spec = pl.BlockSpec((H, kt, tm), lambda i, e: (0, e, i))
    tok_spec = pl.BlockSpec((H, PEER_N_KEYS, tm), lambda i, e: (0, 0, i), **once)
    return pl.pallas_call(
        functools.partial(_peer_dense_kernel, tm=tm, te=te),
        grid=(T // tm, E // te),
        in_specs=[pl.BlockSpec((tm, D), lambda i, e: (i, 0), **once),
                  pl.BlockSpec((te, D), lambda i, e: (e, 0)),
                  pl.BlockSpec((D, te), lambda i, e: (0, e)),
                  key_spec, key_spec, tok_spec, tok_spec,
                  pl.BlockSpec((tm, D), lambda i, e: (i, 0), **once),
                  pl.BlockSpec((1, D), lambda i, e: (0, 0), **once)],
        out_specs=pl.BlockSpec((tm, D), lambda i, e: (i, 0)),
        out_shape=jax.ShapeDtypeStruct((T, D), F32),
        scratch_shapes=[pltpu.VMEM((D, tm), F32),
                        pltpu.VMEM((te, tm), F32),
                        pltpu.VMEM((te, tm), BF16)],
        compiler_params=_params("parallel", "arbitrary"),
        name="peer_dense",
    )(xn, u, vt, thr, e1, s2, e2, res, g.reshape(1, D))


def _rmsnorm_kernel(x_ref, g_ref, o_ref):
    x = x_ref[...]
    ms = jnp.mean(x * x, axis=-1, keepdims=True)
    o_ref[...] = (x * lax.rsqrt(ms + RMS_EPS) * g_ref[...]).astype(o_ref.dtype)


def _rmsnorm(x, g, out_dtype, *, tm=512):
    M, K = x.shape
    return pl.pallas_call(
        _rmsnorm_kernel,
        grid=(M // tm,),
        in_specs=[pl.BlockSpec((tm, K), lambda i: (i, 0)),
                  pl.BlockSpec((1, K), lambda i: (0, 0))],
        out_specs=pl.BlockSpec((tm, K), lambda i: (i, 0)),
        out_shape=jax.ShapeDtypeStruct((M, K), out_dtype),
        compiler_params=_params("parallel"),
        name="rmsnorm",
    )(x, g.reshape(1, K))


def kernel(x, mem, norm_mix_g, w_in, conv_dw_w, conv_dw_b, conv_ln_g, conv_ln_b,
           lambda_q1, lambda_k1, lambda_q2, lambda_k2, diff_subln_g, w_out,
           norm_cross_g, norm_mem_g, w_cq, w_ckv, w_co,
           norm_peer_g, w_pq, peer_keys1, peer_keys2, peer_u, peer_v, final_norm_g):
    B, S, D = x.shape
    T = B * S
    mem_len = mem.shape[1]
    depth = w_in.shape[0]
    conv_w = conv_dw_w.shape[2]
    n_conv = 2 * conv_w
    qk_w = DIFF_HEADS * 2 * DIFF_HEAD_DIM

    assert depth == 1, "the final norm is fused into the PEER kernel of the only layer"
    h = x.reshape(T, D)
    for l in range(depth):
        lam_init = 0.8 - 0.6 * math.exp(-0.3 * l)
        w_conv = w_in[l, :, :n_conv].astype(BF16)
        w_q = (w_in[l, :, n_conv:n_conv + qk_w] * (DIFF_HEAD_DIM ** -0.5)).astype(BF16)
        w_kv = w_in[l, :, n_conv + qk_w:].astype(BF16)
        w_qkv = jnp.concatenate([w_q, w_kv], axis=1)
        zc = _norm_matmul(h, norm_mix_g[l], w_conv, F32, tm=1024, tn=512, name="in_proj_conv")
        zqkv = _norm_matmul(h, norm_mix_g[l], w_qkv, BF16, tm=1024, tn=512, name="in_proj_qkv")
        conv_out = _conv_group(zc, conv_dw_w[l], conv_dw_b[l], conv_ln_g[l], conv_ln_b[l],
                               batch=B, seq=S)
        attn_out = _diff_attention(zqkv, lambda_q1[l], lambda_k1[l], lambda_q2[l], lambda_k2[l],
                                   diff_subln_g[l], batch=B, seq=S, lam_init=lam_init)
        w_o = w_out[l].astype(BF16)
        h = _matmul_residual([(conv_out, w_o[:conv_w]), (attn_out, w_o[conv_w:])], h,
                             tm=1024, tn=512, name="out_proj")
        kv = _norm_matmul(mem.reshape(B * mem_len, D), norm_mem_g[l], w_ckv[l].astype(BF16),
                          BF16, tm=B * mem_len, tn=512, name="cross_kv_proj")
        cq = _norm_matmul(h, norm_cross_g[l], w_cq[l].astype(BF16), BF16,
                          tm=1024, tn=512, name="cross_q_proj")
        co = _cross_attention(cq, kv, batch=B, seq=S, mem_len=mem_len)
        h = _matmul_residual([(co, w_co[l].astype(BF16))], h, tm=1024, tn=512,
                             name="cross_out_proj")
        xn = _rmsnorm(h, norm_peer_g[l], BF16)
        pq = _norm_matmul(h, norm_peer_g[l], w_pq[l].astype(BF16), BF16,
                          tm=1024, tn=512, name="peer_q_proj")
        thr, e1, s2, e2 = _peer_route(pq, peer_keys1[l].astype(BF16), peer_keys2[l].astype(BF16))
        h = _peer_dense(xn, peer_u[l].astype(BF16), peer_v[l].T.astype(BF16),
                        thr, e1, s2, e2, h, final_norm_g)
    return h.reshape(B, S, D)
```

```python
import functools
import math

import jax
import jax.numpy as jnp
from jax import lax
from jax.experimental import pallas as pl
from jax.experimental.pallas import tpu as pltpu

F32 = jnp.float32
BF16 = jnp.bfloat16

RMS_EPS = 1e-6
LN_EPS = 1e-5
NEG_INF = -1e30

LANES = 128
CONV_KERNEL = 31
CONV_HALO = 32
DIFF_HEADS = 8
DIFF_HEAD_DIM = 64
CROSS_HEADS = 4
PEER_HEADS = 8
PEER_N_KEYS = 128
PEER_TOPK = 16
VMEM_LIMIT = 56 * 1024 * 1024

_NT = (((1,), (1,)), ((), ()))


def _params(*sem):
    return pltpu.CompilerParams(dimension_semantics=sem, vmem_limit_bytes=VMEM_LIMIT)


def _norm_mm_kernel(x_ref, g_ref, w_ref, o_ref, hn_ref):
    @pl.when(pl.program_id(1) == 0)
    def _():
        x = x_ref[...]
        ms = jnp.mean(x * x, axis=-1, keepdims=True)
        hn_ref[...] = (x * lax.rsqrt(ms + RMS_EPS) * g_ref[...]).astype(BF16)

    o_ref[...] = jnp.dot(hn_ref[...], w_ref[...],
                         preferred_element_type=F32).astype(o_ref.dtype)


def _norm_matmul(x, g, w, out_dtype, *, tm, tn, name):
    M, K = x.shape
    N = w.shape[1]
    return pl.pallas_call(
        _norm_mm_kernel,
        grid=(M // tm, N // tn),
        in_specs=[pl.BlockSpec((tm, K), lambda i, j: (i, 0)),
                  pl.BlockSpec((1, K), lambda i, j: (0, 0)),
                  pl.BlockSpec((K, tn), lambda i, j: (0, j))],
        out_specs=pl.BlockSpec((tm, tn), lambda i, j: (i, j)),
        out_shape=jax.ShapeDtypeStruct((M, N), out_dtype),
        scratch_shapes=[pltpu.VMEM((tm, K), BF16)],
        compiler_params=_params("parallel", "arbitrary"),
        name=name,
    )(x, g.reshape(1, K), w)


def _mm_res_kernel(*refs, n_pairs):
    res_ref, o_ref = refs[2 * n_pairs], refs[2 * n_pairs + 1]
    acc = res_ref[...]
    for p in range(n_pairs):
        acc = acc + jnp.dot(refs[2 * p][...], refs[2 * p + 1][...],
                            preferred_element_type=F32)
    o_ref[...] = acc


def _matmul_residual(pairs, res, *, tm, tn, name):
    M, N = res.shape
    in_specs, args = [], []
    for a, w in pairs:
        k = a.shape[1]
        in_specs += [pl.BlockSpec((tm, k), lambda i, j: (i, 0)),
                     pl.BlockSpec((k, tn), lambda i, j: (0, j))]
        args += [a, w]
    in_specs.append(pl.BlockSpec((tm, tn), lambda i, j: (i, j)))
    return pl.pallas_call(
        functools.partial(_mm_res_kernel, n_pairs=len(pairs)),
        grid=(M // tm, N // tn),
        in_specs=in_specs,
        out_specs=pl.BlockSpec((tm, tn), lambda i, j: (i, j)),
        out_shape=jax.ShapeDtypeStruct((M, N), F32),
        compiler_params=_params("parallel", "arbitrary"),
        name=name,
    )(*args, res)


def _conv_kernel(a_ref, gt_ref, ah_ref, gh_ref, w_ref, b_ref, lg_ref, lb_ref, o_ref, ubuf,
                 *, ts, rows):
    i = pl.program_id(2)
    uh = ah_ref[...] * jax.nn.sigmoid(gh_ref[...])
    ubuf[0:CONV_HALO, :] = jnp.where(i > 0, uh, 0.0)
    ubuf[CONV_HALO:, :] = a_ref[...] * jax.nn.sigmoid(gt_ref[...])
    bias = b_ref[...]
    lg = lg_ref[...]
    lb = lb_ref[...]
    first = CONV_HALO - (CONV_KERNEL - 1)
    for r in range(ts // rows):
        acc = jnp.zeros((rows, LANES), F32) + bias
        for j in range(CONV_KERNEL):
            start = r * rows + first + j
            acc = acc + w_ref[j:j + 1, :] * ubuf[start:start + rows, :]
        mu = jnp.mean(acc, axis=-1, keepdims=True)
        d = acc - mu
        var = jnp.mean(d * d, axis=-1, keepdims=True)
        y = d * lax.rsqrt(var + LN_EPS) * lg + lb
        o_ref[r * rows:(r + 1) * rows, :] = (y * jax.nn.sigmoid(y)).astype(o_ref.dtype)


def _conv_group(zc, w_dw, b_dw, ln_g, ln_b, *, batch, seq, ts=512, rows=128):
    T, C2 = zc.shape
    C = C2 // 2
    G = C // LANES
    ns = seq // ts
    hb = ts // CONV_HALO

    def main(off):
        return pl.BlockSpec((ts, LANES), lambda b, g, i: (b * ns + i, g + off))

    def halo(off):
        return pl.BlockSpec(
            (CONV_HALO, LANES),
            lambda b, g, i: (jnp.maximum((b * ns + i) * hb - 1, 0), g + off))

    vec = pl.BlockSpec((1, LANES), lambda b, g, i: (0, g))
    return pl.pallas_call(
        functools.partial(_conv_kernel, ts=ts, rows=rows),
        grid=(batch, G, ns),
        in_specs=[main(0), main(G), halo(0), halo(G),
                  pl.BlockSpec((CONV_KERNEL, LANES), lambda b, g, i: (0, g)),
                  vec, vec, vec],
        out_specs=pl.BlockSpec((ts, LANES), lambda b, g, i: (b * ns + i, g)),
        out_shape=jax.ShapeDtypeStruct((T, C), BF16),
        scratch_shapes=[pltpu.VMEM((ts + CONV_HALO, LANES), F32)],
        compiler_params=_params("parallel", "parallel", "arbitrary"),
        name="conv_group",
    )(zc, zc, zc, zc, w_dw, b_dw.reshape(1, C), ln_g.reshape(1, C), ln_b.reshape(1, C))


def _diff_attn_kernel(q_ref, k_ref, v_ref, lq1, lk1, lq2, lk2, sg_ref, o_ref,
                      qs_ref, vx_ref, m_ref, acc_ref, *, tq, rows, lam_init):
    qi = pl.program_id(2)
    per = tq // rows
    n_groups = 2 * per

    @pl.when(qi == 0)
    def _():
        vx_ref[:, :LANES] = v_ref[...]
        vx_ref[:, LANES:] = jnp.ones((vx_ref.shape[0], LANES), BF16)

    q = q_ref[...]
    lane = lax.broadcasted_iota(jnp.int32, (tq, LANES), 1)
    zero = jnp.zeros_like(q)
    for c, qc in enumerate((jnp.where(lane < DIFF_HEAD_DIM, q, zero),
                            jnp.where(lane >= DIFF_HEAD_DIM, q, zero))):
        for r in range(per):
            qs_ref[c * per + r] = qc[r * rows:(r + 1) * rows]
    m_ref[...] = jnp.full(m_ref.shape, NEG_INF, F32)
    acc_ref[...] = jnp.zeros(acc_ref.shape, F32)

    def step(ki, masked):
        start = pl.multiple_of(ki * tq, tq)
        kb = k_ref[pl.ds(start, tq), :]
        vb = vx_ref[pl.ds(start, tq), :]
        for g in range(n_groups):
            s = lax.dot_general(qs_ref[g], kb, _NT, preferred_element_type=F32)
            if masked:
                row = lax.broadcasted_iota(jnp.int32, (rows, tq), 0) + (g % per) * rows
                col = lax.broadcasted_iota(jnp.int32, (rows, tq), 1)
                s = jnp.where(col <= row, s, NEG_INF)
            m_old = m_ref[g]
            m_new = jnp.maximum(m_old, jnp.max(s, axis=-1, keepdims=True))
            alpha = jnp.exp2(m_old - m_new)
            p = jnp.exp2(s - jnp.tile(m_new, (1, tq // LANES)))
            acc_ref[g] = (jnp.tile(alpha, (1, 2)) * acc_ref[g]
                          + jnp.dot(p.astype(BF16), vb, preferred_element_type=F32))
            m_ref[g] = m_new

    def body(ki, c):
        step(ki, False)
        return c

    lax.fori_loop(0, qi, body, 0)
    step(qi, True)

    lam = (jnp.exp(jnp.sum(lq1[...] * lk1[...], axis=-1, keepdims=True))
           - jnp.exp(jnp.sum(lq2[...] * lk2[...], axis=-1, keepdims=True)) + lam_init)
    for r in range(per):
        a1, a2 = acc_ref[r], acc_ref[per + r]
        of = a1[:, :LANES] / a1[:, LANES:] - lam * (a2[:, :LANES] / a2[:, LANES:])
        of = of * lax.rsqrt(jnp.mean(of * of, axis=-1, keepdims=True) + RMS_EPS)
        of = of * sg_ref[...] * (1.0 - lam_init)
        o_ref[r * rows:(r + 1) * rows, :] = of.astype(o_ref.dtype)


def _diff_attention(zqkv, lq1, lk1, lq2, lk2, subln_g, *, batch, seq, lam_init, tq=512, rows=256):
    T = zqkv.shape[0]
    H = DIFF_HEADS
    nq = seq // tq
    n_groups = 2 * tq // rows
    lam_spec = pl.BlockSpec((1, DIFF_HEAD_DIM), lambda b, h, i: (0, 0))
    return pl.pallas_call(
        functools.partial(_diff_attn_kernel, tq=tq, rows=rows, lam_init=lam_init),
        grid=(batch, H, nq),
        in_specs=[pl.BlockSpec((tq, LANES), lambda b, h, i: (b * nq + i, h)),
                  pl.BlockSpec((seq, LANES), lambda b, h, i: (b, H + h)),
                  pl.BlockSpec((seq, LANES), lambda b, h, i: (b, 2 * H + h)),
                  lam_spec, lam_spec, lam_spec, lam_spec,
                  pl.BlockSpec((1, LANES), lambda b, h, i: (0, 0))],
        out_specs=pl.BlockSpec((tq, LANES), lambda b, h, i: (b * nq + i, h)),
        out_shape=jax.ShapeDtypeStruct((T, H * LANES), BF16),
        scratch_shapes=[pltpu.VMEM((n_groups, rows, LANES), BF16),
                        pltpu.VMEM((seq, 2 * LANES), BF16),
                        pltpu.VMEM((n_groups, rows, LANES), F32),
                        pltpu.VMEM((n_groups, rows, 2 * LANES), F32)],
        compiler_params=_params("parallel", "parallel", "arbitrary"),
        name="diff_attention",
    )(zqkv, zqkv, zqkv,
      lq1.reshape(1, -1), lk1.reshape(1, -1), lq2.reshape(1, -1), lk2.reshape(1, -1),
      subln_g.reshape(1, -1))


def _cross_attn_kernel(q_ref, k_ref, v_ref, o_ref, *, dh, scale):
    for h in range(CROSS_HEADS):
        sl = slice(h * dh, (h + 1) * dh)
        s = lax.dot_general(q_ref[:, sl], k_ref[:, sl], _NT,
                            preferred_element_type=F32) * scale
        p = jnp.exp(s - jnp.max(s, axis=-1, keepdims=True))
        l = jnp.sum(p, axis=-1, keepdims=True)
        o = jnp.dot(p.astype(BF16), v_ref[:, sl], preferred_element_type=F32)
        o_ref[:, sl] = (o / l).astype(o_ref.dtype)


def _cross_attention(q, kv, *, batch, seq, mem_len, tq=512):
    T, D = q.shape
    dh = D // CROSS_HEADS
    nq = seq // tq
    return pl.pallas_call(
        functools.partial(_cross_attn_kernel, dh=dh, scale=dh ** -0.5),
        grid=(batch, nq),
        in_specs=[pl.BlockSpec((tq, D), lambda b, i: (b * nq + i, 0)),
                  pl.BlockSpec((mem_len, D), lambda b, i: (b, 0)),
                  pl.BlockSpec((mem_len, D), lambda b, i: (b, 1))],
        out_specs=pl.BlockSpec((tq, D), lambda b, i: (b * nq + i, 0)),
        out_shape=jax.ShapeDtypeStruct((T, D), BF16),
        compiler_params=_params("parallel", "arbitrary"),
        name="cross_attention",
    )(q, kv, kv)


def _top_values(work, n):
    vals = []
    for _ in range(n):
        m = jnp.max(work, axis=0, keepdims=True)
        vals.append(m)
        work = jnp.where(work == m, -jnp.inf, work)
    return vals


def _peer_route_kernel(q_ref, k1_ref, k2_ref, thr_ref, e1_ref, s2_ref, e2_ref):
    K = PEER_TOPK
    q = q_ref[...]
    s1 = lax.dot_general(k1_ref[0], q[:, :PEER_N_KEYS], _NT, preferred_element_type=F32)
    s2 = lax.dot_general(k2_ref[0], q[:, PEER_N_KEYS:], _NT, preferred_element_type=F32)
    v1 = _top_values(s1, K + 1)
    v2 = _top_values(s2, K + 1)
    v1a, v1b = jnp.concatenate(v1[:8], axis=0), jnp.concatenate(v1[8:16], axis=0)
    v2a, v2b = jnp.concatenate(v2[:8], axis=0), jnp.concatenate(v2[8:16], axis=0)
    ninf = jnp.full_like(v1[0], -jnp.inf)
    cand = [v1[0] + v2a, v1[0] + v2b]
    cand += [v1[a] + v2a for a in range(1, 8)]
    cand += [v1b + v2[0]]
    cand += [jnp.concatenate([v1[K] + v2[0], v1[0] + v2[K]] + [ninf] * 6, axis=0)]
    top = _top_values(jnp.concatenate(cand, axis=0), K + 1)
    z = jnp.ones_like(top[0])
    for k in range(1, K):
        z = z + jnp.exp(top[k] - top[0])
    tau = 0.5 * (top[K - 1] + top[K])
    thr_ref[0] = tau - s1
    e1_ref[0] = jnp.exp(s1 - v1[0]) * (1.0 / z)
    s2_ref[0] = s2
    e2_ref[0] = jnp.exp(s2 - v2[0])


def _peer_route(q, keys1, keys2, *, tm=256):
    T = q.shape[0]
    H = PEER_HEADS
    kspec = pl.BlockSpec((1, PEER_N_KEYS, PEER_N_KEYS), lambda i, h: (h, 0, 0))
    ospec = pl.BlockSpec((1, PEER_N_KEYS, tm), lambda i, h: (h, 0, i))
    oshape = jax.ShapeDtypeStruct((H, PEER_N_KEYS, T), F32)
    return pl.pallas_call(
        _peer_route_kernel,
        grid=(T // tm, H),
        in_specs=[pl.BlockSpec((tm, 2 * PEER_N_KEYS), lambda i, h: (i, h)), kspec, kspec],
        out_specs=[ospec, ospec, ospec, ospec],
        out_shape=[oshape, oshape, oshape, oshape],
        compiler_params=_params("parallel", "arbitrary"),
        name="peer_route",
    )(q, keys1, keys2)


def _peer_dense_kernel(xn_ref, u_ref, vt_ref, thr_ref, e1_ref, s2_ref, e2_ref, res_ref, g_ref,
                       o_ref, yt_ref, ht_ref, wg_ref, *, tm, te, chain):
    e = pl.program_id(1)

    @pl.when(e == 0)
    def _():
        yt_ref[...] = jnp.zeros(yt_ref.shape, F32)

    for c0 in range(0, tm, chain):
        tok = slice(c0, c0 + chain)
        ht_ref[:, tok] = lax.dot_general(u_ref[...], xn_ref[tok, :], _NT,
                                         preferred_element_type=F32)
        for il in range(te // LANES):
            rs = slice(il * LANES, (il + 1) * LANES)
            for lt in range(c0, c0 + chain, LANES):
                cs = slice(lt, lt + LANES)
                gate = jnp.zeros((LANES, LANES), F32)
                for h in range(PEER_HEADS):
                    thr = thr_ref[h, il:il + 1, cs]
                    e1 = e1_ref[h, il:il + 1, cs]
                    gate = gate + jnp.where(s2_ref[h, :, cs] >= thr, e2_ref[h, :, cs], 0.0) * e1
                hh = ht_ref[rs, cs]
                act = 0.5 * hh * (1.0 + lax.erf(hh * (2.0 ** -0.5)))
                wg_ref[rs, cs] = (gate * act).astype(BF16)
        yt_ref[:, tok] += jnp.dot(vt_ref[...], wg_ref[:, tok], preferred_element_type=F32)

    @pl.when(e == pl.num_programs(1) - 1)
    def _():
        h = res_ref[...] + yt_ref[...].T
        ms = jnp.mean(h * h, axis=-1, keepdims=True)
        o_ref[...] = h * lax.rsqrt(ms + RMS_EPS) * g_ref[...]


def _peer_dense(xn, u, vt, thr, e1, s2, e2, res, g, *, tm=512, te=1024, chain=512):
    T, D = xn.shape
    E = u.shape[0]
    H = PEER_HEADS
    kt = te // LANES
    once = dict(pipeline_mode=pl.Buffered(1))
    key_spec = pl.BlockSpec((H, kt, tm), lambda i, e: (0, e, i))
    tok_spec = pl.BlockSpec((H, PEER_N_KEYS, tm), lambda i, e: (0, 0, i), **once)
    return pl.pallas_call(
        functools.partial(_peer_dense_kernel, tm=tm, te=te, chain=chain),
        grid=(T // tm, E // te),
        in_specs=[pl.BlockSpec((tm, D), lambda i, e: (i, 0), **once),
                  pl.BlockSpec((te, D), lambda i, e: (e, 0)),
                  pl.BlockSpec((D, te), lambda i, e: (0, e)),
                  key_spec, key_spec, tok_spec, tok_spec,
                  pl.BlockSpec((tm, D), lambda i, e: (i, 0), **once),
                  pl.BlockSpec((1, D), lambda i, e: (0, 0), **once)],
        out_specs=pl.BlockSpec((tm, D), lambda i, e: (i, 0)),
        out_shape=jax.ShapeDtypeStruct((T, D), F32),
        scratch_shapes=[pltpu.VMEM((D, tm), F32),
                        pltpu.VMEM((te, tm), F32),
                        pltpu.VMEM((te, tm), BF16)],
        compiler_params=_params("parallel", "arbitrary"),
        name="peer_dense",
    )(xn, u, vt, thr, e1, s2, e2, res, g.reshape(1, D))


def _rmsnorm_kernel(x_ref, g_ref, o_ref):
    x = x_ref[...]
    ms = jnp.mean(x * x, axis=-1, keepdims=True)
    o_ref[...] = (x * lax.rsqrt(ms + RMS_EPS) * g_ref[...]).astype(o_ref.dtype)


def _rmsnorm(x, g, out_dtype, *, tm=512):
    M, K = x.shape
    return pl.pallas_call(
        _rmsnorm_kernel,
        grid=(M // tm,),
        in_specs=[pl.BlockSpec((tm, K), lambda i: (i, 0)),
                  pl.BlockSpec((1, K), lambda i: (0, 0))],
        out_specs=pl.BlockSpec((tm, K), lambda i: (i, 0)),
        out_shape=jax.ShapeDtypeStruct((M, K), out_dtype),
        compiler_params=_params("parallel"),
        name="rmsnorm",
    )(x, g.reshape(1, K))


def kernel(x, mem, norm_mix_g, w_in, conv_dw_w, conv_dw_b, conv_ln_g, conv_ln_b,
           lambda_q1, lambda_k1, lambda_q2, lambda_k2, diff_subln_g, w_out,
           norm_cross_g, norm_mem_g, w_cq, w_ckv, w_co,
           norm_peer_g, w_pq, peer_keys1, peer_keys2, peer_u, peer_v, final_norm_g):
    B, S, D = x.shape
    T = B * S
    mem_len = mem.shape[1]
    depth = w_in.shape[0]
    conv_w = conv_dw_w.shape[2]
    n_conv = 2 * conv_w
    qk_w = DIFF_HEADS * 2 * DIFF_HEAD_DIM

    assert depth == 1, "the final norm is fused into the PEER kernel of the only layer"
    h = x.reshape(T, D)
    for l in range(depth):
        lam_init = 0.8 - 0.6 * math.exp(-0.3 * l)
        w_conv = w_in[l, :, :n_conv].astype(BF16)
        w_q = (w_in[l, :, n_conv:n_conv + qk_w]
               * (DIFF_HEAD_DIM ** -0.5 * math.log2(math.e))).astype(BF16)
        w_kv = w_in[l, :, n_conv + qk_w:].astype(BF16)
        w_qkv = jnp.concatenate([w_q, w_kv], axis=1)
        zc = _norm_matmul(h, norm_mix_g[l], w_conv, F32, tm=1024, tn=512, name="in_proj_conv")
        zqkv = _norm_matmul(h, norm_mix_g[l], w_qkv, BF16, tm=1024, tn=512, name="in_proj_qkv")
        conv_out = _conv_group(zc, conv_dw_w[l], conv_dw_b[l], conv_ln_g[l], conv_ln_b[l],
                               batch=B, seq=S)
        attn_out = _diff_attention(zqkv, lambda_q1[l], lambda_k1[l], lambda_q2[l], lambda_k2[l],
                                   diff_subln_g[l], batch=B, seq=S, lam_init=lam_init)
        w_o = w_out[l].astype(BF16)
        h = _matmul_residual([(conv_out, w_o[:conv_w]), (attn_out, w_o[conv_w:])], h,
                             tm=1024, tn=512, name="out_proj")
        kv = _norm_matmul(mem.reshape(B * mem_len, D), norm_mem_g[l], w_ckv[l].astype(BF16),
                          BF16, tm=B * mem_len, tn=512, name="cross_kv_proj")
        cq = _norm_matmul(h, norm_cross_g[l], w_cq[l].astype(BF16), BF16,
                          tm=1024, tn=512, name="cross_q_proj")
        co = _cross_attention(cq, kv, batch=B, seq=S, mem_len=mem_len)
        h = _matmul_residual([(co, w_co[l].astype(BF16))], h, tm=1024, tn=512,
                             name="cross_out_proj")
        xn = _rmsnorm(h, norm_peer_g[l], BF16)
        pq = _norm_matmul(h, norm_peer_g[l], w_pq[l].astype(BF16), BF16,
                          tm=1024, tn=512, name="peer_q_proj")
        thr, e1, s2, e2 = _peer_route(pq, peer_keys1[l].astype(BF16), peer_keys2[l].astype(BF16))
        h = _peer_dense(xn, peer_u[l].astype(BF16), peer_v[l].T.astype(BF16),
                        thr, e1, s2, e2, h, final_norm_g)
    return h.reshape(B, S, D)
```

```python
import functools
import math

import jax
import jax.numpy as jnp
from jax import lax
from jax.experimental import pallas as pl
from jax.experimental.pallas import tpu as pltpu

F32 = jnp.float32
BF16 = jnp.bfloat16

RMS_EPS = 1e-6
LN_EPS = 1e-5
NEG_INF = -1e30

LANES = 128
CONV_KERNEL = 31
CONV_HALO = 32
DIFF_HEADS = 8
DIFF_HEAD_DIM = 64
CROSS_HEADS = 4
PEER_HEADS = 8
PEER_N_KEYS = 128
PEER_TOPK = 16
VMEM_LIMIT = 56 * 1024 * 1024

_NT = (((1,), (1,)), ((), ()))


def _params(*sem):
    return pltpu.CompilerParams(dimension_semantics=sem, vmem_limit_bytes=VMEM_LIMIT)


def _norm_mm_kernel(x_ref, g_ref, w_ref, o_ref, hn_ref):
    @pl.when(pl.program_id(1) == 0)
    def _():
        x = x_ref[...]
        ms = jnp.mean(x * x, axis=-1, keepdims=True)
        hn_ref[...] = (x * lax.rsqrt(ms + RMS_EPS) * g_ref[...]).astype(BF16)

    o_ref[...] = jnp.dot(hn_ref[...], w_ref[...],
                         preferred_element_type=F32).astype(o_ref.dtype)


def _norm_matmul(x, g, w, out_dtype, *, tm, tn, name):
    M, K = x.shape
    N = w.shape[1]
    return pl.pallas_call(
        _norm_mm_kernel,
        grid=(M // tm, N // tn),
        in_specs=[pl.BlockSpec((tm, K), lambda i, j: (i, 0)),
                  pl.BlockSpec((1, K), lambda i, j: (0, 0)),
                  pl.BlockSpec((K, tn), lambda i, j: (0, j))],
        out_specs=pl.BlockSpec((tm, tn), lambda i, j: (i, j)),
        out_shape=jax.ShapeDtypeStruct((M, N), out_dtype),
        scratch_shapes=[pltpu.VMEM((tm, K), BF16)],
        compiler_params=_params("parallel", "arbitrary"),
        name=name,
    )(x, g.reshape(1, K), w)


def _mm_res_kernel(*refs, n_pairs):
    res_ref, o_ref = refs[2 * n_pairs], refs[2 * n_pairs + 1]
    acc = res_ref[...]
    for p in range(n_pairs):
        acc = acc + jnp.dot(refs[2 * p][...], refs[2 * p + 1][...],
                            preferred_element_type=F32)
    o_ref[...] = acc


def _matmul_residual(pairs, res, *, tm, tn, name):
    M, N = res.shape
    in_specs, args = [], []
    for a, w in pairs:
        k = a.shape[1]
        in_specs += [pl.BlockSpec((tm, k), lambda i, j: (i, 0)),
                     pl.BlockSpec((k, tn), lambda i, j: (0, j))]
        args += [a, w]
    in_specs.append(pl.BlockSpec((tm, tn), lambda i, j: (i, j)))
    return pl.pallas_call(
        functools.partial(_mm_res_kernel, n_pairs=len(pairs)),
        grid=(M // tm, N // tn),
        in_specs=in_specs,
        out_specs=pl.BlockSpec((tm, tn), lambda i, j: (i, j)),
        out_shape=jax.ShapeDtypeStruct((M, N), F32),
        compiler_params=_params("parallel", "arbitrary"),
        name=name,
    )(*args, res)


def _conv_kernel(a_ref, gt_ref, ah_ref, gh_ref, w_ref, b_ref, lg_ref, lb_ref, o_ref, ubuf,
                 *, ts, rows):
    i = pl.program_id(2)
    uh = ah_ref[...] * jax.nn.sigmoid(gh_ref[...])
    ubuf[0:CONV_HALO, :] = jnp.where(i > 0, uh, 0.0)
    ubuf[CONV_HALO:, :] = a_ref[...] * jax.nn.sigmoid(gt_ref[...])
    bias = b_ref[...]
    lg = lg_ref[...]
    lb = lb_ref[...]
    first = CONV_HALO - (CONV_KERNEL - 1)
    for r in range(ts // rows):
        acc = jnp.zeros((rows, LANES), F32) + bias
        for j in range(CONV_KERNEL):
            start = r * rows + first + j
            acc = acc + w_ref[j:j + 1, :] * ubuf[start:start + rows, :]
        mu = jnp.mean(acc, axis=-1, keepdims=True)
        d = acc - mu
        var = jnp.mean(d * d, axis=-1, keepdims=True)
        y = d * lax.rsqrt(var + LN_EPS) * lg + lb
        o_ref[r * rows:(r + 1) * rows, :] = (y * jax.nn.sigmoid(y)).astype(o_ref.dtype)


def _conv_group(zc, w_dw, b_dw, ln_g, ln_b, *, batch, seq, ts=512, rows=128):
    T, C2 = zc.shape
    C = C2 // 2
    G = C // LANES
    ns = seq // ts
    hb = ts // CONV_HALO

    def main(off):
        return pl.BlockSpec((ts, LANES), lambda b, g, i: (b * ns + i, g + off))

    def halo(off):
        return pl.BlockSpec(
            (CONV_HALO, LANES),
            lambda b, g, i: (jnp.maximum((b * ns + i) * hb - 1, 0), g + off))

    vec = pl.BlockSpec((1, LANES), lambda b, g, i: (0, g))
    return pl.pallas_call(
        functools.partial(_conv_kernel, ts=ts, rows=rows),
        grid=(batch, G, ns),
        in_specs=[main(0), main(G), halo(0), halo(G),
                  pl.BlockSpec((CONV_KERNEL, LANES), lambda b, g, i: (0, g)),
                  vec, vec, vec],
        out_specs=pl.BlockSpec((ts, LANES), lambda b, g, i: (b * ns + i, g)),
        out_shape=jax.ShapeDtypeStruct((T, C), BF16),
        scratch_shapes=[pltpu.VMEM((ts + CONV_HALO, LANES), F32)],
        compiler_params=_params("parallel", "parallel", "arbitrary"),
        name="conv_group",
    )(zc, zc, zc, zc, w_dw, b_dw.reshape(1, C), ln_g.reshape(1, C), ln_b.reshape(1, C))


def _diff_attn_kernel(q_ref, k_ref, v_ref, lq1, lk1, lq2, lk2, sg_ref, o_ref,
                      qs_ref, vx_ref, m_ref, acc_ref, *, tq, rows, lam_init):
    qi = pl.program_id(2)
    per = tq // rows
    n_groups = 2 * per

    @pl.when(qi == 0)
    def _():
        vx_ref[:, :LANES] = v_ref[...]
        vx_ref[:, LANES:] = jnp.ones((vx_ref.shape[0], LANES), BF16)

    q = q_ref[...]
    lane = lax.broadcasted_iota(jnp.int32, (tq, LANES), 1)
    zero = jnp.zeros_like(q)
    for c, qc in enumerate((jnp.where(lane < DIFF_HEAD_DIM, q, zero),
                            jnp.where(lane >= DIFF_HEAD_DIM, q, zero))):
        for r in range(per):
            qs_ref[c * per + r] = qc[r * rows:(r + 1) * rows]
    m_ref[...] = jnp.full(m_ref.shape, NEG_INF, F32)
    acc_ref[...] = jnp.zeros(acc_ref.shape, F32)

    def step(ki, masked):
        start = pl.multiple_of(ki * tq, tq)
        kb = k_ref[pl.ds(start, tq), :]
        vb = vx_ref[pl.ds(start, tq), :]
        for g in range(n_groups):
            s = lax.dot_general(qs_ref[g], kb, _NT, preferred_element_type=F32)
            if masked:
                row = lax.broadcasted_iota(jnp.int32, (rows, tq), 0) + (g % per) * rows
                col = lax.broadcasted_iota(jnp.int32, (rows, tq), 1)
                s = jnp.where(col <= row, s, NEG_INF)
            m_old = m_ref[g]
            m_new = jnp.maximum(m_old, jnp.max(s, axis=-1, keepdims=True))
            alpha = jnp.exp2(m_old - m_new)
            p = jnp.exp2(s - jnp.tile(m_new, (1, tq // LANES)))
            acc_ref[g] = (jnp.tile(alpha, (1, 2)) * acc_ref[g]
                          + jnp.dot(p.astype(BF16), vb, preferred_element_type=F32))
            m_ref[g] = m_new

    def body(ki, c):
        step(ki, False)
        return c

    lax.fori_loop(0, qi, body, 0)
    step(qi, True)

    lam = (jnp.exp(jnp.sum(lq1[...] * lk1[...], axis=-1, keepdims=True))
           - jnp.exp(jnp.sum(lq2[...] * lk2[...], axis=-1, keepdims=True)) + lam_init)
    for r in range(per):
        a1, a2 = acc_ref[r], acc_ref[per + r]
        of = a1[:, :LANES] / a1[:, LANES:] - lam * (a2[:, :LANES] / a2[:, LANES:])
        of = of * lax.rsqrt(jnp.mean(of * of, axis=-1, keepdims=True) + RMS_EPS)
        of = of * sg_ref[...] * (1.0 - lam_init)
        o_ref[r * rows:(r + 1) * rows, :] = of.astype(o_ref.dtype)


def _diff_attention(zqkv, lq1, lk1, lq2, lk2, subln_g, *, batch, seq, lam_init, tq=512, rows=256):
    T = zqkv.shape[0]
    H = DIFF_HEADS
    nq = seq // tq
    n_groups = 2 * tq // rows
    lam_spec = pl.BlockSpec((1, DIFF_HEAD_DIM), lambda b, h, i: (0, 0))
    return pl.pallas_call(
        functools.partial(_diff_attn_kernel, tq=tq, rows=rows, lam_init=lam_init),
        grid=(batch, H, nq),
        in_specs=[pl.BlockSpec((tq, LANES), lambda b, h, i: (b * nq + i, h)),
                  pl.BlockSpec((seq, LANES), lambda b, h, i: (b, H + h)),
                  pl.BlockSpec((seq, LANES), lambda b, h, i: (b, 2 * H + h)),
                  lam_spec, lam_spec, lam_spec, lam_spec,
                  pl.BlockSpec((1, LANES), lambda b, h, i: (0, 0))],
        out_specs=pl.BlockSpec((tq, LANES), lambda b, h, i: (b * nq + i, h)),
        out_shape=jax.ShapeDtypeStruct((T, H * LANES), BF16),
        scratch_shapes=[pltpu.VMEM((n_groups, rows, LANES), BF16),
                        pltpu.VMEM((seq, 2 * LANES), BF16),
                        pltpu.VMEM((n_groups, rows, LANES), F32),
                        pltpu.VMEM((n_groups, rows, 2 * LANES), F32)],
        compiler_params=_params("parallel", "parallel", "arbitrary"),
        name="diff_attention",
    )(zqkv, zqkv, zqkv,
      lq1.reshape(1, -1), lk1.reshape(1, -1), lq2.reshape(1, -1), lk2.reshape(1, -1),
      subln_g.reshape(1, -1))


def _cross_attn_kernel(q_ref, k_ref, v_ref, o_ref, *, dh, scale):
    for h in range(CROSS_HEADS):
        sl = slice(h * dh, (h + 1) * dh)
        s = lax.dot_general(q_ref[:, sl], k_ref[:, sl], _NT,
                            preferred_element_type=F32) * scale
        p = jnp.exp(s - jnp.max(s, axis=-1, keepdims=True))
        l = jnp.sum(p, axis=-1, keepdims=True)
        o = jnp.dot(p.astype(BF16), v_ref[:, sl], preferred_element_type=F32)
        o_ref[:, sl] = (o / l).astype(o_ref.dtype)


def _cross_attention(q, kv, *, batch, seq, mem_len, tq=512):
    T, D = q.shape
    dh = D // CROSS_HEADS
    nq = seq // tq
    return pl.pallas_call(
        functools.partial(_cross_attn_kernel, dh=dh, scale=dh ** -0.5),
        grid=(batch, nq),
        in_specs=[pl.BlockSpec((tq, D), lambda b, i: (b * nq + i, 0)),
                  pl.BlockSpec((mem_len, D), lambda b, i: (b, 0)),
                  pl.BlockSpec((mem_len, D), lambda b, i: (b, 1))],
        out_specs=pl.BlockSpec((tq, D), lambda b, i: (b * nq + i, 0)),
        out_shape=jax.ShapeDtypeStruct((T, D), BF16),
        compiler_params=_params("parallel", "arbitrary"),
        name="cross_attention",
    )(q, kv, kv)


def _top_values(work, n):
    vals = []
    for _ in range(n):
        m = jnp.max(work, axis=0, keepdims=True)
        vals.append(m)
        work = jnp.where(work == m, -jnp.inf, work)
    return vals


def _peer_route_kernel(q_ref, k1_ref, k2_ref, thr_ref, e1_ref, e2_ref):
    K = PEER_TOPK
    q = q_ref[...]
    s1 = lax.dot_general(k1_ref[0], q[:, :PEER_N_KEYS], _NT, preferred_element_type=F32)
    s2 = lax.dot_general(k2_ref[0], q[:, PEER_N_KEYS:], _NT, preferred_element_type=F32)
    v1 = _top_values(s1, K + 1)
    v2 = _top_values(s2, K + 1)
    v1a, v1b = jnp.concatenate(v1[:8], axis=0), jnp.concatenate(v1[8:16], axis=0)
    v2a, v2b = jnp.concatenate(v2[:8], axis=0), jnp.concatenate(v2[8:16], axis=0)
    ninf = jnp.full_like(v1[0], -jnp.inf)
    cand = [v1[0] + v2a, v1[0] + v2b]
    cand += [v1[a] + v2a for a in range(1, 8)]
    cand += [v1b + v2[0]]
    cand += [jnp.concatenate([v1[K] + v2[0], v1[0] + v2[K]] + [ninf] * 6, axis=0)]
    top = _top_values(jnp.concatenate(cand, axis=0), K + 1)
    z = jnp.ones_like(top[0])
    for k in range(1, K):
        z = z + jnp.exp(top[k] - top[0])
    tau = 0.5 * (top[K - 1] + top[K])
    thr_ref[0] = jnp.exp((tau - v2[0]) - s1)
    e1_ref[0] = jnp.exp(s1 - v1[0]) * (1.0 / z)
    e2_ref[0] = jnp.exp(s2 - v2[0])


def _peer_route(q, keys1, keys2, *, tm=256):
    T = q.shape[0]
    H = PEER_HEADS
    kspec = pl.BlockSpec((1, PEER_N_KEYS, PEER_N_KEYS), lambda i, h: (h, 0, 0))
    ospec = pl.BlockSpec((1, PEER_N_KEYS, tm), lambda i, h: (h, 0, i))
    oshape = jax.ShapeDtypeStruct((H, PEER_N_KEYS, T), F32)
    return pl.pallas_call(
        _peer_route_kernel,
        grid=(T // tm, H),
        in_specs=[pl.BlockSpec((tm, 2 * PEER_N_KEYS), lambda i, h: (i, h)), kspec, kspec],
        out_specs=[ospec, ospec, ospec],
        out_shape=[oshape, oshape, oshape],
        compiler_params=_params("parallel", "arbitrary"),
        name="peer_route",
    )(q, keys1, keys2)


def _peer_dense_kernel(xn_ref, u_ref, vt_ref, thr_c, thr_p, e1_c, e1_p, e2_c, e2_p,
                       res_ref, g_ref, o_ref, yt_ref, ht0, ht1, wg0, wg1, wg0_prev,
                       *, tm, half, n_exp, n_blocks, n_sub):
    g = pl.program_id(0)
    d_model = yt_ref.shape[0]
    kh = half // LANES
    prev = jnp.maximum(g - 1, 0)

    @pl.when(g == 0)
    def _():
        ht1[...] = jnp.zeros(ht1.shape, F32)
        wg0_prev[...] = jnp.zeros(wg0_prev.shape, BF16)

    @pl.when(prev % n_exp == 0)
    def _():
        yt_ref[...] = jnp.zeros(yt_ref.shape, F32)

    first_valid = (g >= 1).astype(F32)
    second_valid = (g <= n_blocks - 1).astype(F32)

    def pair_step(u_rows, ht_w, ht_r, wg_w, wg_r, thr_ref, e1_ref, e2_ref, key0, vt_cols, valid):
        half_valid = 0.5 * valid
        ra, rc = half // n_sub, d_model // n_sub
        for j in range(n_sub):
            ht_w[j * ra:(j + 1) * ra, :] = lax.dot_general(
                u_ref[u_rows + j * ra:u_rows + (j + 1) * ra, :], xn_ref[...], _NT,
                preferred_element_type=F32)
            for il in range(j * ra // LANES, (j + 1) * ra // LANES):
                rs = slice(il * LANES, (il + 1) * LANES)
                for lt in range(tm // LANES):
                    cs = slice(lt * LANES, (lt + 1) * LANES)
                    gate = jnp.zeros((LANES, LANES), F32)
                    for h in range(PEER_HEADS):
                        e2v = e2_ref[h, :, cs]
                        thr = thr_ref[h, key0 + il:key0 + il + 1, cs]
                        e1 = e1_ref[h, key0 + il:key0 + il + 1, cs]
                        gate = gate + jnp.where(e2v >= thr, e2v, 0.0) * e1
                    hh = ht_r[rs, cs]
                    act = (half_valid * hh) * (1.0 + lax.erf(hh * (2.0 ** -0.5)))
                    wg_w[rs, cs] = (gate * act).astype(BF16)
            yt_ref[j * rc:(j + 1) * rc, :] += jnp.dot(
                vt_ref[j * rc:(j + 1) * rc, vt_cols:vt_cols + half], wg_r[...],
                preferred_element_type=F32)

    pair_step(0, ht0, ht1, wg1, wg0_prev, thr_p, e1_p, e2_p, kh, 0, first_valid)
    pair_step(half, ht1, ht0, wg0, wg1, thr_c, e1_c, e2_c, 0, half, second_valid)

    @pl.when(g < n_blocks)
    def _():
        wg0_prev[...] = wg0[...]

    @pl.when(jnp.logical_and(g >= 1, prev % n_exp == n_exp - 1))
    def _():
        h = res_ref[...] + yt_ref[...].T
        ms = jnp.mean(h * h, axis=-1, keepdims=True)
        o_ref[...] = h * lax.rsqrt(ms + RMS_EPS) * g_ref[...]


def _peer_dense(xn, u, vt, thr, e1, e2, res, g, *, tm=512, te=1024, n_sub=2):
    T, D = xn.shape
    E = u.shape[0]
    H = PEER_HEADS
    kt = te // LANES
    n_exp = E // te
    n_blocks = (T // tm) * n_exp

    def cur(g):
        b = jnp.minimum(g, n_blocks - 1)
        return b // n_exp, b % n_exp

    def prev(g):
        b = jnp.maximum(g - 1, 0)
        return b // n_exp, b % n_exp

    once = dict(pipeline_mode=pl.Buffered(1))

    def key_spec(f):
        return pl.BlockSpec((H, kt, tm), lambda g: (0, f(g)[1], f(g)[0]))

    def tok_spec(f):
        return pl.BlockSpec((H, PEER_N_KEYS, tm), lambda g: (0, 0, f(g)[0]), **once)

    return pl.pallas_call(
        functools.partial(_peer_dense_kernel, tm=tm, half=te // 2, n_exp=n_exp,
                          n_blocks=n_blocks, n_sub=n_sub),
        grid=(n_blocks + 1,),
        in_specs=[pl.BlockSpec((tm, D), lambda g: (cur(g)[0], 0), **once),
                  pl.BlockSpec((te, D), lambda g: (cur(g)[1], 0)),
                  pl.BlockSpec((D, te), lambda g: (0, prev(g)[1])),
                  key_spec(cur), key_spec(prev), key_spec(cur), key_spec(prev),
                  tok_spec(cur), tok_spec(prev),
                  pl.BlockSpec((tm, D), lambda g: (prev(g)[0], 0), **once),
                  pl.BlockSpec((1, D), lambda g: (0, 0), **once)],
        out_specs=pl.BlockSpec((tm, D), lambda g: (prev(g)[0], 0)),
        out_shape=jax.ShapeDtypeStruct((T, D), F32),
        scratch_shapes=[pltpu.VMEM((D, tm), F32),
                        pltpu.VMEM((te // 2, tm), F32), pltpu.VMEM((te // 2, tm), F32),
                        pltpu.VMEM((te // 2, tm), BF16), pltpu.VMEM((te // 2, tm), BF16),
                        pltpu.VMEM((te // 2, tm), BF16)],
        compiler_params=_params("arbitrary"),
        name="peer_dense",
    )(xn, u, vt, thr, thr, e1, e1, e2, e2, res, g.reshape(1, D))


def _rmsnorm_kernel(x_ref, g_ref, o_ref):
    x = x_ref[...]
    ms = jnp.mean(x * x, axis=-1, keepdims=True)
    o_ref[...] = (x * lax.rsqrt(ms + RMS_EPS) * g_ref[...]).astype(o_ref.dtype)


def _rmsnorm(x, g, out_dtype, *, tm=512):
    M, K = x.shape
    return pl.pallas_call(
        _rmsnorm_kernel,
        grid=(M // tm,),
        in_specs=[pl.BlockSpec((tm, K), lambda i: (i, 0)),
                  pl.BlockSpec((1, K), lambda i: (0, 0))],
        out_specs=pl.BlockSpec((tm, K), lambda i: (i, 0)),
        out_shape=jax.ShapeDtypeStruct((M, K), out_dtype),
        compiler_params=_params("parallel"),
        name="rmsnorm",
    )(x, g.reshape(1, K))


def kernel(x, mem, norm_mix_g, w_in, conv_dw_w, conv_dw_b, conv_ln_g, conv_ln_b,
           lambda_q1, lambda_k1, lambda_q2, lambda_k2, diff_subln_g, w_out,
           norm_cross_g, norm_mem_g, w_cq, w_ckv, w_co,
           norm_peer_g, w_pq, peer_keys1, peer_keys2, peer_u, peer_v, final_norm_g):
    B, S, D = x.shape
    T = B * S
    mem_len = mem.shape[1]
    depth = w_in.shape[0]
    conv_w = conv_dw_w.shape[2]
    n_conv = 2 * conv_w
    qk_w = DIFF_HEADS * 2 * DIFF_HEAD_DIM

    assert depth == 1, "the final norm is fused into the PEER kernel of the only layer"
    h = x.reshape(T, D)
    for l in range(depth):
        lam_init = 0.8 - 0.6 * math.exp(-0.3 * l)
        w_conv = w_in[l, :, :n_conv].astype(BF16)
        w_q = (w_in[l, :, n_conv:n_conv + qk_w]
               * (DIFF_HEAD_DIM ** -0.5 * math.log2(math.e))).astype(BF16)
        w_kv = w_in[l, :, n_conv + qk_w:].astype(BF16)
        w_qkv = jnp.concatenate([w_q, w_kv], axis=1)
        zc = _norm_matmul(h, norm_mix_g[l], w_conv, F32, tm=1024, tn=512, name="in_proj_conv")
        zqkv = _norm_matmul(h, norm_mix_g[l], w_qkv, BF16, tm=1024, tn=512, name="in_proj_qkv")
        conv_out = _conv_group(zc, conv_dw_w[l], conv_dw_b[l], conv_ln_g[l], conv_ln_b[l],
                               batch=B, seq=S)
        attn_out = _diff_attention(zqkv, lambda_q1[l], lambda_k1[l], lambda_q2[l], lambda_k2[l],
                                   diff_subln_g[l], batch=B, seq=S, lam_init=lam_init)
        w_o = w_out[l].astype(BF16)
        h = _matmul_residual([(conv_out, w_o[:conv_w]), (attn_out, w_o[conv_w:])], h,
                             tm=1024, tn=512, name="out_proj")
        kv = _norm_matmul(mem.reshape(B * mem_len, D), norm_mem_g[l], w_ckv[l].astype(BF16),
                          BF16, tm=B * mem_len, tn=512, name="cross_kv_proj")
        cq = _norm_matmul(h, norm_cross_g[l], w_cq[l].astype(BF16), BF16,
                          tm=1024, tn=512, name="cross_q_proj")
        co = _cross_attention(cq, kv, batch=B, seq=S, mem_len=mem_len)
        h = _matmul_residual([(co, w_co[l].astype(BF16))], h, tm=1024, tn=512,
                             name="cross_out_proj")
        xn = _rmsnorm(h, norm_peer_g[l], BF16)
        pq = _norm_matmul(h, norm_peer_g[l], w_pq[l].astype(BF16), BF16,
                          tm=1024, tn=512, name="peer_q_proj")
        thr, e1, e2 = _peer_route(pq, peer_keys1[l].astype(BF16), peer_keys2[l].astype(BF16))
        h = _peer_dense(xn, peer_u[l].astype(BF16), peer_v[l].T.astype(BF16),
                        thr, e1, e2, h, final_norm_g)
    return h.reshape(B, S, D)
```

```python
import functools
import math

import jax
import jax.numpy as jnp
from jax import lax
from jax.experimental import pallas as pl
from jax.experimental.pallas import tpu as pltpu

F32 = jnp.float32
BF16 = jnp.bfloat16

RMS_EPS = 1e-6
LN_EPS = 1e-5
NEG_INF = -1e30

LANES = 128
CONV_KERNEL = 31
CONV_HALO = 32
DIFF_HEADS = 8
DIFF_HEAD_DIM = 64
CROSS_HEADS = 4
PEER_HEADS = 8
PEER_N_KEYS = 128
PEER_TOPK = 16
VMEM_LIMIT = 56 * 1024 * 1024

_NT = (((1,), (1,)), ((), ()))


def _params(*sem):
    return pltpu.CompilerParams(dimension_semantics=sem, vmem_limit_bytes=VMEM_LIMIT)


def _norm_mm_kernel(x_ref, g_ref, w_ref, o_ref, hn_ref):
    @pl.when(pl.program_id(1) == 0)
    def _():
        x = x_ref[...]
        ms = jnp.mean(x * x, axis=-1, keepdims=True)
        hn_ref[...] = (x * lax.rsqrt(ms + RMS_EPS) * g_ref[...]).astype(BF16)

    o_ref[...] = jnp.dot(hn_ref[...], w_ref[...],
                         preferred_element_type=F32).astype(o_ref.dtype)


def _norm_matmul(x, g, w, out_dtype, *, tm, tn, name):
    M, K = x.shape
    N = w.shape[1]
    return pl.pallas_call(
        _norm_mm_kernel,
        grid=(M // tm, N // tn),
        in_specs=[pl.BlockSpec((tm, K), lambda i, j: (i, 0)),
                  pl.BlockSpec((1, K), lambda i, j: (0, 0)),
                  pl.BlockSpec((K, tn), lambda i, j: (0, j))],
        out_specs=pl.BlockSpec((tm, tn), lambda i, j: (i, j)),
        out_shape=jax.ShapeDtypeStruct((M, N), out_dtype),
        scratch_shapes=[pltpu.VMEM((tm, K), BF16)],
        compiler_params=_params("parallel", "arbitrary"),
        name=name,
    )(x, g.reshape(1, K), w)


def _mm_res_kernel(*refs, n_pairs):
    res_ref, o_ref = refs[2 * n_pairs], refs[2 * n_pairs + 1]
    acc = res_ref[...]
    for p in range(n_pairs):
        acc = acc + jnp.dot(refs[2 * p][...], refs[2 * p + 1][...],
                            preferred_element_type=F32)
    o_ref[...] = acc


def _matmul_residual(pairs, res, *, tm, tn, name):
    M, N = res.shape
    in_specs, args = [], []
    for a, w in pairs:
        k = a.shape[1]
        in_specs += [pl.BlockSpec((tm, k), lambda i, j: (i, 0)),
                     pl.BlockSpec((k, tn), lambda i, j: (0, j))]
        args += [a, w]
    in_specs.append(pl.BlockSpec((tm, tn), lambda i, j: (i, j)))
    return pl.pallas_call(
        functools.partial(_mm_res_kernel, n_pairs=len(pairs)),
        grid=(M // tm, N // tn),
        in_specs=in_specs,
        out_specs=pl.BlockSpec((tm, tn), lambda i, j: (i, j)),
        out_shape=jax.ShapeDtypeStruct((M, N), F32),
        compiler_params=_params("parallel", "arbitrary"),
        name=name,
    )(*args, res)


def _conv_kernel(a_ref, gt_ref, ah_ref, gh_ref, w_ref, b_ref, lg_ref, lb_ref, o_ref, ubuf,
                 *, ts, rows):
    i = pl.program_id(2)
    uh = ah_ref[...] * jax.nn.sigmoid(gh_ref[...])
    ubuf[0:CONV_HALO, :] = jnp.where(i > 0, uh, 0.0)
    ubuf[CONV_HALO:, :] = a_ref[...] * jax.nn.sigmoid(gt_ref[...])
    bias = b_ref[...]
    lg = lg_ref[...]
    lb = lb_ref[...]
    first = CONV_HALO - (CONV_KERNEL - 1)
    for r in range(ts // rows):
        acc = jnp.zeros((rows, LANES), F32) + bias
        for j in range(CONV_KERNEL):
            start = r * rows + first + j
            acc = acc + w_ref[j:j + 1, :] * ubuf[start:start + rows, :]
        mu = jnp.mean(acc, axis=-1, keepdims=True)
        d = acc - mu
        var = jnp.mean(d * d, axis=-1, keepdims=True)
        y = d * lax.rsqrt(var + LN_EPS) * lg + lb
        o_ref[r * rows:(r + 1) * rows, :] = (y * jax.nn.sigmoid(y)).astype(o_ref.dtype)


def _conv_group(zc, w_dw, b_dw, ln_g, ln_b, *, batch, seq, ts=512, rows=128):
    T, C2 = zc.shape
    C = C2 // 2
    G = C // LANES
    ns = seq // ts
    hb = ts // CONV_HALO

    def main(off):
        return pl.BlockSpec((ts, LANES), lambda b, g, i: (b * ns + i, g + off))

    def halo(off):
        return pl.BlockSpec(
            (CONV_HALO, LANES),
            lambda b, g, i: (jnp.maximum((b * ns + i) * hb - 1, 0), g + off))

    vec = pl.BlockSpec((1, LANES), lambda b, g, i: (0, g))
    return pl.pallas_call(
        functools.partial(_conv_kernel, ts=ts, rows=rows),
        grid=(batch, G, ns),
        in_specs=[main(0), main(G), halo(0), halo(G),
                  pl.BlockSpec((CONV_KERNEL, LANES), lambda b, g, i: (0, g)),
                  vec, vec, vec],
        out_specs=pl.BlockSpec((ts, LANES), lambda b, g, i: (b * ns + i, g)),
        out_shape=jax.ShapeDtypeStruct((T, C), BF16),
        scratch_shapes=[pltpu.VMEM((ts + CONV_HALO, LANES), F32)],
        compiler_params=_params("parallel", "parallel", "arbitrary"),
        name="conv_group",
    )(zc, zc, zc, zc, w_dw, b_dw.reshape(1, C), ln_g.reshape(1, C), ln_b.reshape(1, C))


def _diff_attn_kernel(q_ref, k_ref, v_ref, lq1, lk1, lq2, lk2, sg_ref, o_ref,
                      qs_ref, vx_ref, m_ref, acc_ref, *, tq, rows, lam_init):
    qi = pl.program_id(2)
    per = tq // rows
    n_groups = 2 * per

    @pl.when(qi == 0)
    def _():
        vx_ref[:, :LANES] = v_ref[...]
        vx_ref[:, LANES:] = jnp.ones((vx_ref.shape[0], LANES), BF16)

    q = q_ref[...]
    lane = lax.broadcasted_iota(jnp.int32, (tq, LANES), 1)
    zero = jnp.zeros_like(q)
    for c, qc in enumerate((jnp.where(lane < DIFF_HEAD_DIM, q, zero),
                            jnp.where(lane >= DIFF_HEAD_DIM, q, zero))):
        for r in range(per):
            qs_ref[c * per + r] = qc[r * rows:(r + 1) * rows]
    m_ref[...] = jnp.full(m_ref.shape, NEG_INF, F32)
    acc_ref[...] = jnp.zeros(acc_ref.shape, F32)

    def step(ki, masked):
        start = pl.multiple_of(ki * tq, tq)
        kb = k_ref[pl.ds(start, tq), :]
        vb = vx_ref[pl.ds(start, tq), :]
        for g in range(n_groups):
            s = lax.dot_general(qs_ref[g], kb, _NT, preferred_element_type=F32)
            if masked:
                row = lax.broadcasted_iota(jnp.int32, (rows, tq), 0) + (g % per) * rows
                col = lax.broadcasted_iota(jnp.int32, (rows, tq), 1)
                s = jnp.where(col <= row, s, NEG_INF)
            m_old = m_ref[g]
            m_new = jnp.maximum(m_old, jnp.max(s, axis=-1, keepdims=True))
            alpha = jnp.exp2(m_old - m_new)
            p = jnp.exp2(s - jnp.tile(m_new, (1, tq // LANES)))
            acc_ref[g] = (jnp.tile(alpha, (1, 2)) * acc_ref[g]
                          + jnp.dot(p.astype(BF16), vb, preferred_element_type=F32))
            m_ref[g] = m_new

    def body(k2, c):
        step(2 * k2, False)
        step(2 * k2 + 1, False)
        return c

    lax.fori_loop(0, qi // 2, body, 0)

    @pl.when(qi % 2 == 1)
    def _():
        step(qi - 1, False)

    step(qi, True)

    lam = (jnp.exp(jnp.sum(lq1[...] * lk1[...], axis=-1, keepdims=True))
           - jnp.exp(jnp.sum(lq2[...] * lk2[...], axis=-1, keepdims=True)) + lam_init)
    for r in range(per):
        a1, a2 = acc_ref[r], acc_ref[per + r]
        of = a1[:, :LANES] / a1[:, LANES:] - lam * (a2[:, :LANES] / a2[:, LANES:])
        of = of * lax.rsqrt(jnp.mean(of * of, axis=-1, keepdims=True) + RMS_EPS)
        of = of * sg_ref[...] * (1.0 - lam_init)
        o_ref[r * rows:(r + 1) * rows, :] = of.astype(o_ref.dtype)


def _diff_attention(zqkv, lq1, lk1, lq2, lk2, subln_g, *, batch, seq, lam_init, tq=512, rows=256):
    T = zqkv.shape[0]
    H = DIFF_HEADS
    nq = seq // tq
    n_groups = 2 * tq // rows
    lam_spec = pl.BlockSpec((1, DIFF_HEAD_DIM), lambda b, h, i: (0, 0))
    return pl.pallas_call(
        functools.partial(_diff_attn_kernel, tq=tq, rows=rows, lam_init=lam_init),
        grid=(batch, H, nq),
        in_specs=[pl.BlockSpec((tq, LANES), lambda b, h, i: (b * nq + i, h)),
                  pl.BlockSpec((seq, LANES), lambda b, h, i: (b, H + h)),
                  pl.BlockSpec((seq, LANES), lambda b, h, i: (b, 2 * H + h)),
                  lam_spec, lam_spec, lam_spec, lam_spec,
                  pl.BlockSpec((1, LANES), lambda b, h, i: (0, 0))],
        out_specs=pl.BlockSpec((tq, LANES), lambda b, h, i: (b * nq + i, h)),
        out_shape=jax.ShapeDtypeStruct((T, H * LANES), BF16),
        scratch_shapes=[pltpu.VMEM((n_groups, rows, LANES), BF16),
                        pltpu.VMEM((seq, 2 * LANES), BF16),
                        pltpu.VMEM((n_groups, rows, LANES), F32),
                        pltpu.VMEM((n_groups, rows, 2 * LANES), F32)],
        compiler_params=_params("parallel", "parallel", "arbitrary"),
        name="diff_attention",
    )(zqkv, zqkv, zqkv,
      lq1.reshape(1, -1), lk1.reshape(1, -1), lq2.reshape(1, -1), lk2.reshape(1, -1),
      subln_g.reshape(1, -1))


def _cross_attn_kernel(q_ref, k_ref, v_ref, o_ref, *, dh, scale):
    for h in range(CROSS_HEADS):
        sl = slice(h * dh, (h + 1) * dh)
        s = lax.dot_general(q_ref[:, sl], k_ref[:, sl], _NT,
                            preferred_element_type=F32) * scale
        p = jnp.exp(s - jnp.max(s, axis=-1, keepdims=True))
        l = jnp.sum(p, axis=-1, keepdims=True)
        o = jnp.dot(p.astype(BF16), v_ref[:, sl], preferred_element_type=F32)
        o_ref[:, sl] = (o / l).astype(o_ref.dtype)


def _cross_attention(q, kv, *, batch, seq, mem_len, tq=512):
    T, D = q.shape
    dh = D // CROSS_HEADS
    nq = seq // tq
    return pl.pallas_call(
        functools.partial(_cross_attn_kernel, dh=dh, scale=dh ** -0.5),
        grid=(batch, nq),
        in_specs=[pl.BlockSpec((tq, D), lambda b, i: (b * nq + i, 0)),
                  pl.BlockSpec((mem_len, D), lambda b, i: (b, 0)),
                  pl.BlockSpec((mem_len, D), lambda b, i: (b, 1))],
        out_specs=pl.BlockSpec((tq, D), lambda b, i: (b * nq + i, 0)),
        out_shape=jax.ShapeDtypeStruct((T, D), BF16),
        compiler_params=_params("parallel", "arbitrary"),
        name="cross_attention",
    )(q, kv, kv)


def _top_values(work, n):
    vals = []
    for _ in range(n):
        m = jnp.max(work, axis=0, keepdims=True)
        vals.append(m)
        work = jnp.where(work == m, -jnp.inf, work)
    return vals


def _peer_route_kernel(q_ref, k1_ref, k2_ref, thr_ref, e1_ref, e2_ref):
    K = PEER_TOPK
    q = q_ref[...]
    s1 = lax.dot_general(k1_ref[0], q[:, :PEER_N_KEYS], _NT, preferred_element_type=F32)
    s2 = lax.dot_general(k2_ref[0], q[:, PEER_N_KEYS:], _NT, preferred_element_type=F32)
    v1 = _top_values(s1, K + 1)
    v2 = _top_values(s2, K + 1)
    v1a, v1b = jnp.concatenate(v1[:8], axis=0), jnp.concatenate(v1[8:16], axis=0)
    v2a, v2b = jnp.concatenate(v2[:8], axis=0), jnp.concatenate(v2[8:16], axis=0)
    ninf = jnp.full_like(v1[0], -jnp.inf)
    cand = [v1[0] + v2a, v1[0] + v2b]
    cand += [v1[a] + v2a for a in range(1, 8)]
    cand += [v1b + v2[0]]
    cand += [jnp.concatenate([v1[K] + v2[0], v1[0] + v2[K]] + [ninf] * 6, axis=0)]
    top = _top_values(jnp.concatenate(cand, axis=0), K + 1)
    z = jnp.ones_like(top[0])
    for k in range(1, K):
        z = z + jnp.exp(top[k] - top[0])
    tau = 0.5 * (top[K - 1] + top[K])
    thr_ref[0] = jnp.exp((tau - v2[0]) - s1)
    e1_ref[0] = jnp.exp(s1 - v1[0]) * (1.0 / z)
    e2_ref[0] = jnp.exp(s2 - v2[0])


def _peer_route(q, keys1, keys2, *, tm=256):
    T = q.shape[0]
    H = PEER_HEADS
    kspec = pl.BlockSpec((1, PEER_N_KEYS, PEER_N_KEYS), lambda i, h: (h, 0, 0))
    ospec = pl.BlockSpec((1, PEER_N_KEYS, tm), lambda i, h: (h, 0, i))
    oshape = jax.ShapeDtypeStruct((H, PEER_N_KEYS, T), F32)
    return pl.pallas_call(
        _peer_route_kernel,
        grid=(T // tm, H),
        in_specs=[pl.BlockSpec((tm, 2 * PEER_N_KEYS), lambda i, h: (i, h)), kspec, kspec],
        out_specs=[ospec, ospec, ospec],
        out_shape=[oshape, oshape, oshape],
        compiler_params=_params("parallel", "arbitrary"),
        name="peer_route",
    )(q, keys1, keys2)


def _peer_dense_kernel(xnt_ref, u_ref, vt_ref, thr_c, thr_p, e1_c, e1_p, e2_c, e2_p,
                       res_ref, g_ref, o_ref, yt_ref, ht0, ht1, wg0, wg1, wg0_prev,
                       *, tm, half, n_exp, n_blocks, n_sub):
    g = pl.program_id(0)
    d_model = yt_ref.shape[0]
    kh = half // LANES
    prev = jnp.maximum(g - 1, 0)

    @pl.when(g == 0)
    def _():
        ht1[...] = jnp.zeros(ht1.shape, F32)
        wg0_prev[...] = jnp.zeros(wg0_prev.shape, BF16)

    @pl.when(prev % n_exp == 0)
    def _():
        yt_ref[...] = jnp.zeros(yt_ref.shape, F32)

    first_valid = (g >= 1).astype(F32)
    second_valid = (g <= n_blocks - 1).astype(F32)

    def pair_step(u_rows, ht_w, ht_r, wg_w, wg_r, thr_ref, e1_ref, e2_ref, key0, vt_cols, valid):
        half_valid = 0.5 * valid
        ra, rc = half // n_sub, d_model // n_sub
        for j in range(n_sub):
            ht_w[j * ra:(j + 1) * ra, :] = jnp.dot(
                u_ref[u_rows + j * ra:u_rows + (j + 1) * ra, :], xnt_ref[...],
                preferred_element_type=F32)
            for il in range(j * ra // LANES, (j + 1) * ra // LANES):
                rs = slice(il * LANES, (il + 1) * LANES)
                for lt in range(tm // LANES):
                    cs = slice(lt * LANES, (lt + 1) * LANES)
                    gate = jnp.zeros((LANES, LANES), F32)
                    for h in range(PEER_HEADS):
                        e2v = e2_ref[h, :, cs]
                        thr = thr_ref[h, key0 + il:key0 + il + 1, cs]
                        e1 = e1_ref[h, key0 + il:key0 + il + 1, cs]
                        gate = gate + jnp.where(e2v >= thr, e2v, 0.0) * e1
                    hh = ht_r[rs, cs]
                    act = (half_valid * hh) * (1.0 + lax.erf(hh * (2.0 ** -0.5)))
                    wg_w[rs, cs] = (gate * act).astype(BF16)
            yt_ref[j * rc:(j + 1) * rc, :] += jnp.dot(
                vt_ref[j * rc:(j + 1) * rc, vt_cols:vt_cols + half], wg_r[...],
                preferred_element_type=F32)

    pair_step(0, ht0, ht1, wg1, wg0_prev, thr_p, e1_p, e2_p, kh, 0, first_valid)
    pair_step(half, ht1, ht0, wg0, wg1, thr_c, e1_c, e2_c, 0, half, second_valid)

    @pl.when(g < n_blocks)
    def _():
        wg0_prev[...] = wg0[...]

    @pl.when(jnp.logical_and(g >= 1, prev % n_exp == n_exp - 1))
    def _():
        h = res_ref[...] + yt_ref[...].T
        ms = jnp.mean(h * h, axis=-1, keepdims=True)
        o_ref[...] = h * lax.rsqrt(ms + RMS_EPS) * g_ref[...]


def _peer_dense(xnt, u, vt, thr, e1, e2, res, g, *, tm=512, te=1024, n_sub=2):
    D, T = xnt.shape
    E = u.shape[0]
    H = PEER_HEADS
    kt = te // LANES
    n_exp = E // te
    n_blocks = (T // tm) * n_exp

    def cur(g):
        b = jnp.minimum(g, n_blocks - 1)
        return b // n_exp, b % n_exp

    def prev(g):
        b = jnp.maximum(g - 1, 0)
        return b // n_exp, b % n_exp

    once = dict(pipeline_mode=pl.Buffered(1))

    def key_spec(f):
        return pl.BlockSpec((H, kt, tm), lambda g: (0, f(g)[1], f(g)[0]))

    def tok_spec(f):
        return pl.BlockSpec((H, PEER_N_KEYS, tm), lambda g: (0, 0, f(g)[0]), **once)

    return pl.pallas_call(
        functools.partial(_peer_dense_kernel, tm=tm, half=te // 2, n_exp=n_exp,
                          n_blocks=n_blocks, n_sub=n_sub),
        grid=(n_blocks + 1,),
        in_specs=[pl.BlockSpec((D, tm), lambda g: (0, cur(g)[0]), **once),
                  pl.BlockSpec((te, D), lambda g: (cur(g)[1], 0)),
                  pl.BlockSpec((D, te), lambda g: (0, prev(g)[1])),
                  key_spec(cur), key_spec(prev), key_spec(cur), key_spec(prev),
                  tok_spec(cur), tok_spec(prev),
                  pl.BlockSpec((tm, D), lambda g: (prev(g)[0], 0), **once),
                  pl.BlockSpec((1, D), lambda g: (0, 0), **once)],
        out_specs=pl.BlockSpec((tm, D), lambda g: (prev(g)[0], 0)),
        out_shape=jax.ShapeDtypeStruct((T, D), F32),
        scratch_shapes=[pltpu.VMEM((D, tm), F32),
                        pltpu.VMEM((te // 2, tm), F32), pltpu.VMEM((te // 2, tm), F32),
                        pltpu.VMEM((te // 2, tm), BF16), pltpu.VMEM((te // 2, tm), BF16),
                        pltpu.VMEM((te // 2, tm), BF16)],
        compiler_params=_params("arbitrary"),
        name="peer_dense",
    )(xnt, u, vt, thr, thr, e1, e1, e2, e2, res, g.reshape(1, D))


def _rmsnorm_t_kernel(x_ref, g_ref, o_ref):
    x = x_ref[...]
    ms = jnp.mean(x * x, axis=-1, keepdims=True)
    o_ref[...] = (x * lax.rsqrt(ms + RMS_EPS) * g_ref[...]).T.astype(o_ref.dtype)


def _rmsnorm_transposed(x, g, out_dtype, *, tm=512):
    M, K = x.shape
    return pl.pallas_call(
        _rmsnorm_t_kernel,
        grid=(M // tm,),
        in_specs=[pl.BlockSpec((tm, K), lambda i: (i, 0)),
                  pl.BlockSpec((1, K), lambda i: (0, 0))],
        out_specs=pl.BlockSpec((K, tm), lambda i: (0, i)),
        out_shape=jax.ShapeDtypeStruct((K, M), out_dtype),
        compiler_params=_params("parallel"),
        name="rmsnorm_t",
    )(x, g.reshape(1, K))


def kernel(x, mem, norm_mix_g, w_in, conv_dw_w, conv_dw_b, conv_ln_g, conv_ln_b,
           lambda_q1, lambda_k1, lambda_q2, lambda_k2, diff_subln_g, w_out,
           norm_cross_g, norm_mem_g, w_cq, w_ckv, w_co,
           norm_peer_g, w_pq, peer_keys1, peer_keys2, peer_u, peer_v, final_norm_g):
    B, S, D = x.shape
    T = B * S
    mem_len = mem.shape[1]
    depth = w_in.shape[0]
    conv_w = conv_dw_w.shape[2]
    n_conv = 2 * conv_w
    qk_w = DIFF_HEADS * 2 * DIFF_HEAD_DIM

    assert depth == 1, "the final norm is fused into the PEER kernel of the only layer"
    h = x.reshape(T, D)
    for l in range(depth):
        lam_init = 0.8 - 0.6 * math.exp(-0.3 * l)
        w_conv = w_in[l, :, :n_conv].astype(BF16)
        w_q = (w_in[l, :, n_conv:n_conv + qk_w]
               * (DIFF_HEAD_DIM ** -0.5 * math.log2(math.e))).astype(BF16)
        w_kv = w_in[l, :, n_conv + qk_w:].astype(BF16)
        w_qkv = jnp.concatenate([w_q, w_kv], axis=1)
        zc = _norm_matmul(h, norm_mix_g[l], w_conv, F32, tm=1024, tn=512, name="in_proj_conv")
        zqkv = _norm_matmul(h, norm_mix_g[l], w_qkv, BF16, tm=1024, tn=512, name="in_proj_qkv")
        conv_out = _conv_group(zc, conv_dw_w[l], conv_dw_b[l], conv_ln_g[l], conv_ln_b[l],
                               batch=B, seq=S)
        attn_out = _diff_attention(zqkv, lambda_q1[l], lambda_k1[l], lambda_q2[l], lambda_k2[l],
                                   diff_subln_g[l], batch=B, seq=S, lam_init=lam_init)
        w_o = w_out[l].astype(BF16)
        h = _matmul_residual([(conv_out, w_o[:conv_w]), (attn_out, w_o[conv_w:])], h,
                             tm=1024, tn=512, name="out_proj")
        kv = _norm_matmul(mem.reshape(B * mem_len, D), norm_mem_g[l], w_ckv[l].astype(BF16),
                          BF16, tm=B * mem_len, tn=512, name="cross_kv_proj")
        cq = _norm_matmul(h, norm_cross_g[l], w_cq[l].astype(BF16), BF16,
                          tm=1024, tn=512, name="cross_q_proj")
        co = _cross_attention(cq, kv, batch=B, seq=S, mem_len=mem_len)
        h = _matmul_residual([(co, w_co[l].astype(BF16))], h, tm=1024, tn=512,
                             name="cross_out_proj")
        xnt = _rmsnorm_transposed(h, norm_peer_g[l], BF16)
        pq = _norm_matmul(h, norm_peer_g[l], w_pq[l].astype(BF16), BF16,
                          tm=1024, tn=512, name="peer_q_proj")
        thr, e1, e2 = _peer_route(pq, peer_keys1[l].astype(BF16), peer_keys2[l].astype(BF16))
        h = _peer_dense(xnt, peer_u[l].astype(BF16), peer_v[l].astype(BF16).T,
                        thr, e1, e2, h, final_norm_g)
    return h.reshape(B, S, D)
```

```python
import functools
import math

import jax
import jax.numpy as jnp
from jax import lax
from jax.experimental import pallas as pl
from jax.experimental.pallas import tpu as pltpu

F32 = jnp.float32
BF16 = jnp.bfloat16

RMS_EPS = 1e-6
LN_EPS = 1e-5
NEG_INF = -1e30

LANES = 128
CONV_KERNEL = 31
CONV_HALO = 32
DIFF_HEADS = 8
DIFF_HEAD_DIM = 64
CROSS_HEADS = 4
PEER_HEADS = 8
PEER_N_KEYS = 128
PEER_TOPK = 16
VMEM_LIMIT = 56 * 1024 * 1024

_NT = (((1,), (1,)), ((), ()))


def _params(*sem):
    return pltpu.CompilerParams(dimension_semantics=sem, vmem_limit_bytes=VMEM_LIMIT)


def _norm_mm_kernel(x_ref, g_ref, w_ref, o_ref, hn_ref):
    @pl.when(pl.program_id(1) == 0)
    def _():
        x = x_ref[...]
        ms = jnp.mean(x * x, axis=-1, keepdims=True)
        hn_ref[...] = (x * lax.rsqrt(ms + RMS_EPS) * g_ref[...]).astype(BF16)

    o_ref[...] = jnp.dot(hn_ref[...], w_ref[...],
                         preferred_element_type=F32).astype(o_ref.dtype)


def _norm_matmul(x, g, w, out_dtype, *, tm, tn, name):
    M, K = x.shape
    N = w.shape[1]
    return pl.pallas_call(
        _norm_mm_kernel,
        grid=(M // tm, N // tn),
        in_specs=[pl.BlockSpec((tm, K), lambda i, j: (i, 0)),
                  pl.BlockSpec((1, K), lambda i, j: (0, 0)),
                  pl.BlockSpec((K, tn), lambda i, j: (0, j))],
        out_specs=pl.BlockSpec((tm, tn), lambda i, j: (i, j)),
        out_shape=jax.ShapeDtypeStruct((M, N), out_dtype),
        scratch_shapes=[pltpu.VMEM((tm, K), BF16)],
        compiler_params=_params("parallel", "arbitrary"),
        name=name,
    )(x, g.reshape(1, K), w)


def _mm_res_kernel(*refs, n_pairs):
    res_ref, o_ref = refs[2 * n_pairs], refs[2 * n_pairs + 1]
    acc = res_ref[...]
    for p in range(n_pairs):
        acc = acc + jnp.dot(refs[2 * p][...], refs[2 * p + 1][...],
                            preferred_element_type=F32)
    o_ref[...] = acc


def _matmul_residual(pairs, res, *, tm, tn, name):
    M, N = res.shape
    in_specs, args = [], []
    for a, w in pairs:
        k = a.shape[1]
        in_specs += [pl.BlockSpec((tm, k), lambda i, j: (i, 0)),
                     pl.BlockSpec((k, tn), lambda i, j: (0, j))]
        args += [a, w]
    in_specs.append(pl.BlockSpec((tm, tn), lambda i, j: (i, j)))
    return pl.pallas_call(
        functools.partial(_mm_res_kernel, n_pairs=len(pairs)),
        grid=(M // tm, N // tn),
        in_specs=in_specs,
        out_specs=pl.BlockSpec((tm, tn), lambda i, j: (i, j)),
        out_shape=jax.ShapeDtypeStruct((M, N), F32),
        compiler_params=_params("parallel", "arbitrary"),
        name=name,
    )(*args, res)


def _conv_kernel(a_ref, gt_ref, ah_ref, gh_ref, w_ref, b_ref, lg_ref, lb_ref, o_ref, ubuf,
                 *, ts, rows):
    i = pl.program_id(2)
    uh = ah_ref[...] * jax.nn.sigmoid(gh_ref[...])
    ubuf[0:CONV_HALO, :] = jnp.where(i > 0, uh, 0.0)
    ubuf[CONV_HALO:, :] = a_ref[...] * jax.nn.sigmoid(gt_ref[...])
    bias = b_ref[...]
    lg = lg_ref[...]
    lb = lb_ref[...]
    first = CONV_HALO - (CONV_KERNEL - 1)
    for r in range(ts // rows):
        acc = jnp.zeros((rows, LANES), F32) + bias
        for j in range(CONV_KERNEL):
            start = r * rows + first + j
            acc = acc + w_ref[j:j + 1, :] * ubuf[start:start + rows, :]
        mu = jnp.mean(acc, axis=-1, keepdims=True)
        d = acc - mu
        var = jnp.mean(d * d, axis=-1, keepdims=True)
        y = d * lax.rsqrt(var + LN_EPS) * lg + lb
        o_ref[r * rows:(r + 1) * rows, :] = (y * jax.nn.sigmoid(y)).astype(o_ref.dtype)


def _conv_group(zc, w_dw, b_dw, ln_g, ln_b, *, batch, seq, ts=512, rows=128):
    T, C2 = zc.shape
    C = C2 // 2
    G = C // LANES
    ns = seq // ts
    hb = ts // CONV_HALO

    def main(off):
        return pl.BlockSpec((ts, LANES), lambda b, g, i: (b * ns + i, g + off))

    def halo(off):
        return pl.BlockSpec(
            (CONV_HALO, LANES),
            lambda b, g, i: (jnp.maximum((b * ns + i) * hb - 1, 0), g + off))

    vec = pl.BlockSpec((1, LANES), lambda b, g, i: (0, g))
    return pl.pallas_call(
        functools.partial(_conv_kernel, ts=ts, rows=rows),
        grid=(batch, G, ns),
        in_specs=[main(0), main(G), halo(0), halo(G),
                  pl.BlockSpec((CONV_KERNEL, LANES), lambda b, g, i: (0, g)),
                  vec, vec, vec],
        out_specs=pl.BlockSpec((ts, LANES), lambda b, g, i: (b * ns + i, g)),
        out_shape=jax.ShapeDtypeStruct((T, C), BF16),
        scratch_shapes=[pltpu.VMEM((ts + CONV_HALO, LANES), F32)],
        compiler_params=_params("parallel", "parallel", "arbitrary"),
        name="conv_group",
    )(zc, zc, zc, zc, w_dw, b_dw.reshape(1, C), ln_g.reshape(1, C), ln_b.reshape(1, C))


def _diff_attn_kernel(q_ref, k_ref, v_ref, lq1, lk1, lq2, lk2, sg_ref, o_ref,
                      qs_ref, vx_ref, m_ref, acc_ref, *, tq, rows, lam_init):
    qi = pl.program_id(2)
    per = tq // rows
    n_groups = 2 * per

    @pl.when(qi == 0)
    def _():
        vx_ref[:, :LANES] = v_ref[...]
        vx_ref[:, LANES:] = jnp.ones((vx_ref.shape[0], LANES), BF16)

    q = q_ref[...]
    lane = lax.broadcasted_iota(jnp.int32, (tq, LANES), 1)
    zero = jnp.zeros_like(q)
    for c, qc in enumerate((jnp.where(lane < DIFF_HEAD_DIM, q, zero),
                            jnp.where(lane >= DIFF_HEAD_DIM, q, zero))):
        for r in range(per):
            qs_ref[c * per + r] = qc[r * rows:(r + 1) * rows]
    m_ref[...] = jnp.full(m_ref.shape, NEG_INF, F32)
    acc_ref[...] = jnp.zeros(acc_ref.shape, F32)

    def step(ki, masked):
        start = pl.multiple_of(ki * tq, tq)
        kb = k_ref[pl.ds(start, tq), :]
        vb = vx_ref[pl.ds(start, tq), :]
        for g in range(n_groups):
            s = lax.dot_general(qs_ref[g], kb, _NT, preferred_element_type=F32)
            if masked:
                row = lax.broadcasted_iota(jnp.int32, (rows, tq), 0) + (g % per) * rows
                col = lax.broadcasted_iota(jnp.int32, (rows, tq), 1)
                s = jnp.where(col <= row, s, NEG_INF)
            m_old = m_ref[g]
            m_new = jnp.maximum(m_old, jnp.max(s, axis=-1, keepdims=True))
            alpha = jnp.exp2(m_old - m_new)
            p = jnp.exp2(s - jnp.tile(m_new, (1, tq // LANES)))
            acc_ref[g] = (jnp.tile(alpha, (1, 2)) * acc_ref[g]
                          + jnp.dot(p.astype(BF16), vb, preferred_element_type=F32))
            m_ref[g] = m_new

    def body(k2, c):
        step(2 * k2, False)
        step(2 * k2 + 1, False)
        return c

    lax.fori_loop(0, qi // 2, body, 0)

    @pl.when(qi % 2 == 1)
    def _():
        step(qi - 1, False)

    step(qi, True)

    lam = (jnp.exp(jnp.sum(lq1[...] * lk1[...], axis=-1, keepdims=True))
           - jnp.exp(jnp.sum(lq2[...] * lk2[...], axis=-1, keepdims=True)) + lam_init)
    for r in range(per):
        a1, a2 = acc_ref[r], acc_ref[per + r]
        of = a1[:, :LANES] / a1[:, LANES:] - lam * (a2[:, :LANES] / a2[:, LANES:])
        of = of * lax.rsqrt(jnp.mean(of * of, axis=-1, keepdims=True) + RMS_EPS)
        of = of * sg_ref[...] * (1.0 - lam_init)
        o_ref[r * rows:(r + 1) * rows, :] = of.astype(o_ref.dtype)


def _diff_attention(zqkv, lq1, lk1, lq2, lk2, subln_g, *, batch, seq, lam_init, tq=512, rows=256):
    T = zqkv.shape[0]
    H = DIFF_HEADS
    nq = seq // tq
    n_groups = 2 * tq // rows
    lam_spec = pl.BlockSpec((1, DIFF_HEAD_DIM), lambda b, h, i: (0, 0))
    return pl.pallas_call(
        functools.partial(_diff_attn_kernel, tq=tq, rows=rows, lam_init=lam_init),
        grid=(batch, H, nq),
        in_specs=[pl.BlockSpec((tq, LANES), lambda b, h, i: (b * nq + i, h)),
                  pl.BlockSpec((seq, LANES), lambda b, h, i: (b, H + h)),
                  pl.BlockSpec((seq, LANES), lambda b, h, i: (b, 2 * H + h)),
                  lam_spec, lam_spec, lam_spec, lam_spec,
                  pl.BlockSpec((1, LANES), lambda b, h, i: (0, 0))],
        out_specs=pl.BlockSpec((tq, LANES), lambda b, h, i: (b * nq + i, h)),
        out_shape=jax.ShapeDtypeStruct((T, H * LANES), BF16),
        scratch_shapes=[pltpu.VMEM((n_groups, rows, LANES), BF16),
                        pltpu.VMEM((seq, 2 * LANES), BF16),
                        pltpu.VMEM((n_groups, rows, LANES), F32),
                        pltpu.VMEM((n_groups, rows, 2 * LANES), F32)],
        compiler_params=_params("parallel", "parallel", "arbitrary"),
        name="diff_attention",
    )(zqkv, zqkv, zqkv,
      lq1.reshape(1, -1), lk1.reshape(1, -1), lq2.reshape(1, -1), lk2.reshape(1, -1),
      subln_g.reshape(1, -1))


def _cross_attn_kernel(q_ref, k_ref, v_ref, o_ref, *, dh, scale):
    for h in range(CROSS_HEADS):
        sl = slice(h * dh, (h + 1) * dh)
        s = lax.dot_general(q_ref[:, sl], k_ref[:, sl], _NT,
                            preferred_element_type=F32) * scale
        p = jnp.exp(s - jnp.max(s, axis=-1, keepdims=True))
        l = jnp.sum(p, axis=-1, keepdims=True)
        o = jnp.dot(p.astype(BF16), v_ref[:, sl], preferred_element_type=F32)
        o_ref[:, sl] = (o / l).astype(o_ref.dtype)


def _cross_attention(q, kv, *, batch, seq, mem_len, tq=512):
    T, D = q.shape
    dh = D // CROSS_HEADS
    nq = seq // tq
    return pl.pallas_call(
        functools.partial(_cross_attn_kernel, dh=dh, scale=dh ** -0.5),
        grid=(batch, nq),
        in_specs=[pl.BlockSpec((tq, D), lambda b, i: (b * nq + i, 0)),
                  pl.BlockSpec((mem_len, D), lambda b, i: (b, 0)),
                  pl.BlockSpec((mem_len, D), lambda b, i: (b, 1))],
        out_specs=pl.BlockSpec((tq, D), lambda b, i: (b * nq + i, 0)),
        out_shape=jax.ShapeDtypeStruct((T, D), BF16),
        compiler_params=_params("parallel", "arbitrary"),
        name="cross_attention",
    )(q, kv, kv)


def _merge_exchange_pairs(n):
    t = max(1, (n - 1).bit_length())
    pairs = []
    p = 1 << (t - 1)
    while p > 0:
        q, r, d = 1 << (t - 1), 0, p
        while d > 0:
            pairs += [(i, i + d) for i in range(n - d) if (i & p) == r]
            d, q, r = q - p, q >> 1, p
        p >>= 1
    return pairs


def _top_values(rows, n):
    s = list(rows)
    for i, j in _merge_exchange_pairs(len(s)):
        s[i], s[j] = jnp.maximum(s[i], s[j]), jnp.minimum(s[i], s[j])
    ninf = jnp.full_like(s[0], -jnp.inf)
    vals = []
    for k in range(n):
        m = jnp.max(s[0], axis=0, keepdims=True)
        vals.append(m)
        hit = s[0] == m
        for r in range(min(len(s), n - k - 1)):
            s[r] = jnp.where(hit, s[r + 1] if r + 1 < len(s) else ninf, s[r])
    return vals


def _peer_route_kernel(q_ref, k1_ref, k2_ref, thr_ref, e1_ref, e2_ref):
    K = PEER_TOPK
    q = q_ref[...]
    s1 = lax.dot_general(k1_ref[0], q[:, :PEER_N_KEYS], _NT, preferred_element_type=F32)
    s2 = lax.dot_general(k2_ref[0], q[:, PEER_N_KEYS:], _NT, preferred_element_type=F32)
    v1 = _top_values([s1[r:r + 8] for r in range(0, PEER_N_KEYS, 8)], K + 1)
    v2 = _top_values([s2[r:r + 8] for r in range(0, PEER_N_KEYS, 8)], K + 1)
    v1a, v1b = jnp.concatenate(v1[:8], axis=0), jnp.concatenate(v1[8:16], axis=0)
    v2a, v2b = jnp.concatenate(v2[:8], axis=0), jnp.concatenate(v2[8:16], axis=0)
    ninf = jnp.full_like(v1[0], -jnp.inf)
    cand = [v1[0] + v2a, v1[0] + v2b]
    cand += [v1[a] + v2a for a in range(1, 8)]
    cand += [v1b + v2[0]]
    cand += [jnp.concatenate([v1[K] + v2[0], v1[0] + v2[K]] + [ninf] * 6, axis=0)]
    top = _top_values(cand, K + 1)
    z = jnp.ones_like(top[0])
    for k in range(1, K):
        z = z + jnp.exp(top[k] - top[0])
    tau = 0.5 * (top[K - 1] + top[K])
    thr_ref[0] = jnp.exp((tau - v2[0]) - s1)
    e1_ref[0] = jnp.exp(s1 - v1[0]) * (1.0 / z)
    e2_ref[0] = jnp.exp(s2 - v2[0])


def _peer_route(q, keys1, keys2, *, tm=1024):
    T = q.shape[0]
    H = PEER_HEADS
    kspec = pl.BlockSpec((1, PEER_N_KEYS, PEER_N_KEYS), lambda i, h: (h, 0, 0))
    ospec = pl.BlockSpec((1, PEER_N_KEYS, tm), lambda i, h: (h, 0, i))
    oshape = jax.ShapeDtypeStruct((H, PEER_N_KEYS, T), F32)
    return pl.pallas_call(
        _peer_route_kernel,
        grid=(T // tm, H),
        in_specs=[pl.BlockSpec((tm, 2 * PEER_N_KEYS), lambda i, h: (i, h)), kspec, kspec],
        out_specs=[ospec, ospec, ospec],
        out_shape=[oshape, oshape, oshape],
        compiler_params=_params("parallel", "arbitrary"),
        name="peer_route",
    )(q, keys1, keys2)


def _peer_dense_kernel(xnt_ref, u_ref, v_ref, thr_c, thr_p, e1_c, e1_p, e2_c, e2_p,
                       res_ref, g_ref, o_ref, y_ref, ht0, ht1, wg0, wg1, wg0_prev,
                       *, tm, half, n_exp, n_blocks, n_sub):
    g = pl.program_id(0)
    d_model = y_ref.shape[1]
    kh = half // LANES
    prev = jnp.maximum(g - 1, 0)

    @pl.when(g == 0)
    def _():
        ht1[...] = jnp.zeros(ht1.shape, F32)
        wg0_prev[...] = jnp.zeros(wg0_prev.shape, BF16)

    @pl.when(prev % n_exp == 0)
    def _():
        y_ref[...] = jnp.zeros(y_ref.shape, F32)

    first_valid = (g >= 1).astype(F32)
    second_valid = (g <= n_blocks - 1).astype(F32)

    def pair_step(u_rows, ht_w, ht_r, wg_w, wg_r, thr_ref, e1_ref, e2_ref, key0, v_rows, valid):
        half_valid = 0.5 * valid
        ra, rc = half // n_sub, d_model // n_sub
        for j in range(n_sub):
            ht_w[j * ra:(j + 1) * ra, :] = jnp.dot(
                u_ref[u_rows + j * ra:u_rows + (j + 1) * ra, :], xnt_ref[...],
                preferred_element_type=F32)
            for il in range(j * ra // LANES, (j + 1) * ra // LANES):
                rs = slice(il * LANES, (il + 1) * LANES)
                for lt in range(tm // LANES):
                    cs = slice(lt * LANES, (lt + 1) * LANES)
                    gate = jnp.zeros((LANES, LANES), F32)
                    for h in range(PEER_HEADS):
                        e2v = e2_ref[h, :, cs]
                        thr = thr_ref[h, key0 + il:key0 + il + 1, cs]
                        e1 = e1_ref[h, key0 + il:key0 + il + 1, cs]
                        gate = gate + jnp.where(e2v >= thr, e2v, 0.0) * e1
                    hh = ht_r[rs, cs]
                    act = (half_valid * hh) * (1.0 + lax.erf(hh * (2.0 ** -0.5)))
                    wg_w[cs, rs] = (gate * act).T.astype(BF16)
            y_ref[:, j * rc:(j + 1) * rc] += jnp.dot(
                wg_r[...], v_ref[v_rows:v_rows + half, j * rc:(j + 1) * rc],
                preferred_element_type=F32)

    pair_step(0, ht0, ht1, wg1, wg0_prev, thr_p, e1_p, e2_p, kh, 0, first_valid)
    pair_step(half, ht1, ht0, wg0, wg1, thr_c, e1_c, e2_c, 0, half, second_valid)

    @pl.when(g < n_blocks)
    def _():
        wg0_prev[...] = wg0[...]

    @pl.when(jnp.logical_and(g >= 1, prev % n_exp == n_exp - 1))
    def _():
        h = res_ref[...] + y_ref[...]
        ms = jnp.mean(h * h, axis=-1, keepdims=True)
        o_ref[...] = h * lax.rsqrt(ms + RMS_EPS) * g_ref[...]


def _peer_dense(xnt, u, v, thr, e1, e2, res, g, *, tm=512, te=1024, n_sub=2):
    D, T = xnt.shape
    E = u.shape[0]
    H = PEER_HEADS
    kt = te // LANES
    n_exp = E // te
    n_blocks = (T // tm) * n_exp

    def cur(g):
        b = jnp.minimum(g, n_blocks - 1)
        return b // n_exp, b % n_exp

    def prev(g):
        b = jnp.maximum(g - 1, 0)
        return b // n_exp, b % n_exp

    once = dict(pipeline_mode=pl.Buffered(1))

    def key_spec(f):
        return pl.BlockSpec((H, kt, tm), lambda g: (0, f(g)[1], f(g)[0]))

    def tok_spec(f):
        return pl.BlockSpec((H, PEER_N_KEYS, tm), lambda g: (0, 0, f(g)[0]), **once)

    return pl.pallas_call(
        functools.partial(_peer_dense_kernel, tm=tm, half=te // 2, n_exp=n_exp,
                          n_blocks=n_blocks, n_sub=n_sub),
        grid=(n_blocks + 1,),
        in_specs=[pl.BlockSpec((D, tm), lambda g: (0, cur(g)[0]), **once),
                  pl.BlockSpec((te, D), lambda g: (cur(g)[1], 0)),
                  pl.BlockSpec((te, D), lambda g: (prev(g)[1], 0)),
                  key_spec(cur), key_spec(prev), key_spec(cur), key_spec(prev),
                  tok_spec(cur), tok_spec(prev),
                  pl.BlockSpec((tm, D), lambda g: (prev(g)[0], 0), **once),
                  pl.BlockSpec((1, D), lambda g: (0, 0), **once)],
        out_specs=pl.BlockSpec((tm, D), lambda g: (prev(g)[0], 0)),
        out_shape=jax.ShapeDtypeStruct((T, D), F32),
        scratch_shapes=[pltpu.VMEM((tm, D), F32),
                        pltpu.VMEM((te // 2, tm), F32), pltpu.VMEM((te // 2, tm), F32),
                        pltpu.VMEM((tm, te // 2), BF16), pltpu.VMEM((tm, te // 2), BF16),
                        pltpu.VMEM((tm, te // 2), BF16)],
        compiler_params=_params("arbitrary"),
        name="peer_dense",
    )(xnt, u, v, thr, thr, e1, e1, e2, e2, res, g.reshape(1, D))


def _rmsnorm_t_kernel(x_ref, g_ref, o_ref):
    x = x_ref[...]
    ms = jnp.mean(x * x, axis=-1, keepdims=True)
    o_ref[...] = (x * lax.rsqrt(ms + RMS_EPS) * g_ref[...]).T.astype(o_ref.dtype)


def _rmsnorm_transposed(x, g, out_dtype, *, tm=512):
    M, K = x.shape
    return pl.pallas_call(
        _rmsnorm_t_kernel,
        grid=(M // tm,),
        in_specs=[pl.BlockSpec((tm, K), lambda i: (i, 0)),
                  pl.BlockSpec((1, K), lambda i: (0, 0))],
        out_specs=pl.BlockSpec((K, tm), lambda i: (0, i)),
        out_shape=jax.ShapeDtypeStruct((K, M), out_dtype),
        compiler_params=_params("parallel"),
        name="rmsnorm_t",
    )(x, g.reshape(1, K))


def kernel(x, mem, norm_mix_g, w_in, conv_dw_w, conv_dw_b, conv_ln_g, conv_ln_b,
           lambda_q1, lambda_k1, lambda_q2, lambda_k2, diff_subln_g, w_out,
           norm_cross_g, norm_mem_g, w_cq, w_ckv, w_co,
           norm_peer_g, w_pq, peer_keys1, peer_keys2, peer_u, peer_v, final_norm_g):
    B, S, D = x.shape
    T = B * S
    mem_len = mem.shape[1]
    depth = w_in.shape[0]
    conv_w = conv_dw_w.shape[2]
    n_conv = 2 * conv_w
    qk_w = DIFF_HEADS * 2 * DIFF_HEAD_DIM

    assert depth == 1, "the final norm is fused into the PEER kernel of the only layer"
    h = x.reshape(T, D)
    for l in range(depth):
        lam_init = 0.8 - 0.6 * math.exp(-0.3 * l)
        w_conv = w_in[l, :, :n_conv].astype(BF16)
        w_q = (w_in[l, :, n_conv:n_conv + qk_w]
               * (DIFF_HEAD_DIM ** -0.5 * math.log2(math.e))).astype(BF16)
        w_kv = w_in[l, :, n_conv + qk_w:].astype(BF16)
        w_qkv = jnp.concatenate([w_q, w_kv], axis=1)
        zc = _norm_matmul(h, norm_mix_g[l], w_conv, F32, tm=1024, tn=512, name="in_proj_conv")
        zqkv = _norm_matmul(h, norm_mix_g[l], w_qkv, BF16, tm=1024, tn=512, name="in_proj_qkv")
        conv_out = _conv_group(zc, conv_dw_w[l], conv_dw_b[l], conv_ln_g[l], conv_ln_b[l],
                               batch=B, seq=S)
        attn_out = _diff_attention(zqkv, lambda_q1[l], lambda_k1[l], lambda_q2[l], lambda_k2[l],
                                   diff_subln_g[l], batch=B, seq=S, lam_init=lam_init)
        w_o = w_out[l].astype(BF16)
        h = _matmul_residual([(conv_out, w_o[:conv_w]), (attn_out, w_o[conv_w:])], h,
                             tm=1024, tn=512, name="out_proj")
        kv = _norm_matmul(mem.reshape(B * mem_len, D), norm_mem_g[l], w_ckv[l].astype(BF16),
                          BF16, tm=B * mem_len, tn=512, name="cross_kv_proj")
        cq = _norm_matmul(h, norm_cross_g[l], w_cq[l].astype(BF16), BF16,
                          tm=1024, tn=512, name="cross_q_proj")
        co = _cross_attention(cq, kv, batch=B, seq=S, mem_len=mem_len)
        h = _matmul_residual([(co, w_co[l].astype(BF16))], h, tm=1024, tn=512,
                             name="cross_out_proj")
        xnt = _rmsnorm_transposed(h, norm_peer_g[l], BF16)
        pq = _norm_matmul(h, norm_peer_g[l], w_pq[l].astype(BF16), BF16,
                          tm=1024, tn=512, name="peer_q_proj")
        thr, e1, e2 = _peer_route(pq, peer_keys1[l].astype(BF16), peer_keys2[l].astype(BF16))
        h = _peer_dense(xnt, peer_u[l].astype(BF16), peer_v[l].astype(BF16),
                        thr, e1, e2, h, final_norm_g)
    return h.reshape(B, S, D)
```

```python
import functools
import math

import jax
import jax.numpy as jnp
from jax import lax
from jax.experimental import pallas as pl
from jax.experimental.pallas import tpu as pltpu

F32 = jnp.float32
BF16 = jnp.bfloat16

RMS_EPS = 1e-6
LN_EPS = 1e-5
NEG_INF = -1e30

LANES = 128
CONV_KERNEL = 31
CONV_HALO = 32
DIFF_HEADS = 8
DIFF_HEAD_DIM = 64
CROSS_HEADS = 4
PEER_HEADS = 8
PEER_N_KEYS = 128
PEER_TOPK = 16
VMEM_LIMIT = 56 * 1024 * 1024

_NT = (((1,), (1,)), ((), ()))


def _params(*sem):
    return pltpu.CompilerParams(dimension_semantics=sem, vmem_limit_bytes=VMEM_LIMIT)


def _norm_mm_kernel(x_ref, g_ref, w_ref, *out_refs, emit_transposed, scaled_tiles, scale):
    o_ref, hn_ref = out_refs[0], out_refs[-1]

    @pl.when(pl.program_id(1) == 0)
    def _():
        x = x_ref[...]
        ms = jnp.mean(x * x, axis=-1, keepdims=True)
        hn = x * lax.rsqrt(ms + RMS_EPS) * g_ref[...]
        hn_ref[...] = hn.astype(BF16)
        if emit_transposed:
            out_refs[1][...] = hn.T.astype(BF16)

    acc = jnp.dot(hn_ref[...], w_ref[...].astype(BF16), preferred_element_type=F32)
    if scaled_tiles:
        acc = acc * jnp.where(pl.program_id(1) < scaled_tiles, scale, 1.0)
    o_ref[...] = acc.astype(o_ref.dtype)


def _norm_matmul(x, g, w, out_dtype, *, tm, tn, name, col0=0, n_cols=None,
                 emit_transposed=False, scaled_tiles=0, scale=1.0):
    M, K = x.shape
    n_cols = w.shape[1] - col0 if n_cols is None else n_cols
    cb0 = col0 // tn
    out_specs = [pl.BlockSpec((tm, tn), lambda i, j: (i, j))]
    out_shape = [jax.ShapeDtypeStruct((M, n_cols), out_dtype)]
    if emit_transposed:
        out_specs.append(pl.BlockSpec((K, tm), lambda i, j: (0, i)))
        out_shape.append(jax.ShapeDtypeStruct((K, M), BF16))
    out = pl.pallas_call(
        functools.partial(_norm_mm_kernel, emit_transposed=emit_transposed,
                          scaled_tiles=scaled_tiles, scale=scale),
        grid=(M // tm, n_cols // tn),
        in_specs=[pl.BlockSpec((tm, K), lambda i, j: (i, 0)),
                  pl.BlockSpec((1, K), lambda i, j: (0, 0)),
                  pl.BlockSpec((K, tn), lambda i, j: (0, j + cb0))],
        out_specs=out_specs,
        out_shape=out_shape,
        scratch_shapes=[pltpu.VMEM((tm, K), BF16)],
        compiler_params=_params("parallel", "arbitrary"),
        name=name,
    )(x, g.reshape(1, K), w)
    return out if emit_transposed else out[0]


def _mm_res_kernel(*refs, n_pairs):
    res_ref, o_ref = refs[2 * n_pairs], refs[2 * n_pairs + 1]
    acc = res_ref[...]
    for p in range(n_pairs):
        acc = acc + jnp.dot(refs[2 * p][...], refs[2 * p + 1][...].astype(BF16),
                            preferred_element_type=F32)
    o_ref[...] = acc


def _matmul_residual(pairs, res, *, tm, tn, name):
    M, N = res.shape
    in_specs, args = [], []
    for a, w, rb in pairs:
        k = a.shape[1]
        in_specs += [pl.BlockSpec((tm, k), lambda i, j: (i, 0)),
                     pl.BlockSpec((k, tn), lambda i, j, rb=rb: (rb, j))]
        args += [a, w]
    in_specs.append(pl.BlockSpec((tm, tn), lambda i, j: (i, j)))
    return pl.pallas_call(
        functools.partial(_mm_res_kernel, n_pairs=len(pairs)),
        grid=(M // tm, N // tn),
        in_specs=in_specs,
        out_specs=pl.BlockSpec((tm, tn), lambda i, j: (i, j)),
        out_shape=jax.ShapeDtypeStruct((M, N), F32),
        compiler_params=_params("parallel", "arbitrary"),
        name=name,
    )(*args, res)


def _conv_kernel(a_ref, gt_ref, ah_ref, gh_ref, w_ref, b_ref, lg_ref, lb_ref, o_ref, ubuf,
                 *, ts, rows):
    i = pl.program_id(2)
    uh = ah_ref[...] * jax.nn.sigmoid(gh_ref[...])
    ubuf[0:CONV_HALO, :] = jnp.where(i > 0, uh, 0.0)
    ubuf[CONV_HALO:, :] = a_ref[...] * jax.nn.sigmoid(gt_ref[...])
    bias = b_ref[...]
    lg = lg_ref[...]
    lb = lb_ref[...]
    first = CONV_HALO - (CONV_KERNEL - 1)
    for r in range(ts // rows):
        acc = jnp.zeros((rows, LANES), F32) + bias
        for j in range(CONV_KERNEL):
            start = r * rows + first + j
            acc = acc + w_ref[j:j + 1, :] * ubuf[start:start + rows, :]
        mu = jnp.mean(acc, axis=-1, keepdims=True)
        d = acc - mu
        var = jnp.mean(d * d, axis=-1, keepdims=True)
        y = d * lax.rsqrt(var + LN_EPS) * lg + lb
        o_ref[r * rows:(r + 1) * rows, :] = (y * jax.nn.sigmoid(y)).astype(o_ref.dtype)


def _conv_group(zc, w_dw, b_dw, ln_g, ln_b, *, batch, seq, ts=512, rows=128):
    T, C2 = zc.shape
    C = C2 // 2
    G = C // LANES
    ns = seq // ts
    hb = ts // CONV_HALO

    def main(off):
        return pl.BlockSpec((ts, LANES), lambda b, g, i: (b * ns + i, g + off))

    def halo(off):
        return pl.BlockSpec(
            (CONV_HALO, LANES),
            lambda b, g, i: (jnp.maximum((b * ns + i) * hb - 1, 0), g + off))

    vec = pl.BlockSpec((1, LANES), lambda b, g, i: (0, g))
    return pl.pallas_call(
        functools.partial(_conv_kernel, ts=ts, rows=rows),
        grid=(batch, G, ns),
        in_specs=[main(0), main(G), halo(0), halo(G),
                  pl.BlockSpec((CONV_KERNEL, LANES), lambda b, g, i: (0, g)),
                  vec, vec, vec],
        out_specs=pl.BlockSpec((ts, LANES), lambda b, g, i: (b * ns + i, g)),
        out_shape=jax.ShapeDtypeStruct((T, C), BF16),
        scratch_shapes=[pltpu.VMEM((ts + CONV_HALO, LANES), F32)],
        compiler_params=_params("parallel", "parallel", "arbitrary"),
        name="conv_group",
    )(zc, zc, zc, zc, w_dw, b_dw.reshape(1, C), ln_g.reshape(1, C), ln_b.reshape(1, C))


def _diff_attn_kernel(q_ref, k_ref, v_ref, lq1, lk1, lq2, lk2, sg_ref, o_ref,
                      qs_ref, vx_ref, m_ref, acc_ref, *, tq, rows, lam_init):
    qi = pl.program_id(2)
    per = tq // rows
    n_groups = 2 * per

    @pl.when(qi == 0)
    def _():
        vx_ref[:, :LANES] = v_ref[...]
        vx_ref[:, LANES:] = jnp.ones((vx_ref.shape[0], LANES), BF16)

    q = q_ref[...]
    lane = lax.broadcasted_iota(jnp.int32, (tq, LANES), 1)
    zero = jnp.zeros_like(q)
    for c, qc in enumerate((jnp.where(lane < DIFF_HEAD_DIM, q, zero),
                            jnp.where(lane >= DIFF_HEAD_DIM, q, zero))):
        for r in range(per):
            qs_ref[c * per + r] = qc[r * rows:(r + 1) * rows]
    m_ref[...] = jnp.full(m_ref.shape, NEG_INF, F32)
    acc_ref[...] = jnp.zeros(acc_ref.shape, F32)

    def step(ki, masked):
        start = pl.multiple_of(ki * tq, tq)
        kb = k_ref[pl.ds(start, tq), :]
        vb = vx_ref[pl.ds(start, tq), :]
        for g in range(n_groups):
            s = lax.dot_general(qs_ref[g], kb, _NT, preferred_element_type=F32)
            if masked:
                row = lax.broadcasted_iota(jnp.int32, (rows, tq), 0) + (g % per) * rows
                col = lax.broadcasted_iota(jnp.int32, (rows, tq), 1)
                s = jnp.where(col <= row, s, NEG_INF)
            m_old = m_ref[g]
            m_new = jnp.maximum(m_old, jnp.max(s, axis=-1, keepdims=True))
            alpha = jnp.exp2(m_old - m_new)
            p = jnp.exp2(s - jnp.tile(m_new, (1, tq // LANES)))
            acc_ref[g] = (jnp.tile(alpha, (1, 2)) * acc_ref[g]
                          + jnp.dot(p.astype(BF16), vb, preferred_element_type=F32))
            m_ref[g] = m_new

    def body(k2, c):
        step(2 * k2, False)
        step(2 * k2 + 1, False)
        return c

    lax.fori_loop(0, qi // 2, body, 0)

    @pl.when(qi % 2 == 1)
    def _():
        step(qi - 1, False)

    step(qi, True)

    lam = (jnp.exp(jnp.sum(lq1[...] * lk1[...], axis=-1, keepdims=True))
           - jnp.exp(jnp.sum(lq2[...] * lk2[...], axis=-1, keepdims=True)) + lam_init)
    for r in range(per):
        a1, a2 = acc_ref[r], acc_ref[per + r]
        of = a1[:, :LANES] / a1[:, LANES:] - lam * (a2[:, :LANES] / a2[:, LANES:])
        of = of * lax.rsqrt(jnp.mean(of * of, axis=-1, keepdims=True) + RMS_EPS)
        of = of * sg_ref[...] * (1.0 - lam_init)
        o_ref[r * rows:(r + 1) * rows, :] = of.astype(o_ref.dtype)


def _diff_attention(zqkv, lq1, lk1, lq2, lk2, subln_g, *, batch, seq, lam_init, tq=512, rows=256):
    T = zqkv.shape[0]
    H = DIFF_HEADS
    nq = seq // tq
    n_groups = 2 * tq // rows
    lam_spec = pl.BlockSpec((1, DIFF_HEAD_DIM), lambda b, h, i: (0, 0))
    return pl.pallas_call(
        functools.partial(_diff_attn_kernel, tq=tq, rows=rows, lam_init=lam_init),
        grid=(batch, H, nq),
        in_specs=[pl.BlockSpec((tq, LANES), lambda b, h, i: (b * nq + i, h)),
                  pl.BlockSpec((seq, LANES), lambda b, h, i: (b, H + h)),
                  pl.BlockSpec((seq, LANES), lambda b, h, i: (b, 2 * H + h)),
                  lam_spec, lam_spec, lam_spec, lam_spec,
                  pl.BlockSpec((1, LANES), lambda b, h, i: (0, 0))],
        out_specs=pl.BlockSpec((tq, LANES), lambda b, h, i: (b * nq + i, h)),
        out_shape=jax.ShapeDtypeStruct((T, H * LANES), BF16),
        scratch_shapes=[pltpu.VMEM((n_groups, rows, LANES), BF16),
                        pltpu.VMEM((seq, 2 * LANES), BF16),
                        pltpu.VMEM((n_groups, rows, LANES), F32),
                        pltpu.VMEM((n_groups, rows, 2 * LANES), F32)],
        compiler_params=_params("parallel", "parallel", "arbitrary"),
        name="diff_attention",
    )(zqkv, zqkv, zqkv,
      lq1.reshape(1, -1), lk1.reshape(1, -1), lq2.reshape(1, -1), lk2.reshape(1, -1),
      subln_g.reshape(1, -1))


def _cross_attn_kernel(q_ref, k_ref, v_ref, o_ref, *, dh, scale):
    for h in range(CROSS_HEADS):
        sl = slice(h * dh, (h + 1) * dh)
        s = lax.dot_general(q_ref[:, sl], k_ref[:, sl], _NT,
                            preferred_element_type=F32) * scale
        p = jnp.exp(s - jnp.max(s, axis=-1, keepdims=True))
        l = jnp.sum(p, axis=-1, keepdims=True)
        o = jnp.dot(p.astype(BF16), v_ref[:, sl], preferred_element_type=F32)
        o_ref[:, sl] = (o / l).astype(o_ref.dtype)


def _cross_attention(q, kv, *, batch, seq, mem_len, tq=512):
    T, D = q.shape
    dh = D // CROSS_HEADS
    nq = seq // tq
    return pl.pallas_call(
        functools.partial(_cross_attn_kernel, dh=dh, scale=dh ** -0.5),
        grid=(batch, nq),
        in_specs=[pl.BlockSpec((tq, D), lambda b, i: (b * nq + i, 0)),
                  pl.BlockSpec((mem_len, D), lambda b, i: (b, 0)),
                  pl.BlockSpec((mem_len, D), lambda b, i: (b, 1))],
        out_specs=pl.BlockSpec((tq, D), lambda b, i: (b * nq + i, 0)),
        out_shape=jax.ShapeDtypeStruct((T, D), BF16),
        compiler_params=_params("parallel", "arbitrary"),
        name="cross_attention",
    )(q, kv, kv)


def _merge_exchange_pairs(n):
    t = max(1, (n - 1).bit_length())
    pairs = []
    p = 1 << (t - 1)
    while p > 0:
        q, r, d = 1 << (t - 1), 0, p
        while d > 0:
            pairs += [(i, i + d) for i in range(n - d) if (i & p) == r]
            d, q, r = q - p, q >> 1, p
        p >>= 1
    return pairs


def _top_values(rows, n):
    s = list(rows)
    for i, j in _merge_exchange_pairs(len(s)):
        s[i], s[j] = jnp.maximum(s[i], s[j]), jnp.minimum(s[i], s[j])
    ninf = jnp.full_like(s[0], -jnp.inf)
    vals = []
    for k in range(n):
        m = jnp.max(s[0], axis=0, keepdims=True)
        vals.append(m)
        hit = s[0] == m
        for r in range(min(len(s), n - k - 1)):
            s[r] = jnp.where(hit, s[r + 1] if r + 1 < len(s) else ninf, s[r])
    return vals


def _peer_route_kernel(q_ref, k1_ref, k2_ref, thr_ref, e1_ref, e2_ref):
    K = PEER_TOPK
    q = q_ref[...]
    s1 = lax.dot_general(k1_ref[0], q[:, :PEER_N_KEYS], _NT, preferred_element_type=F32)
    s2 = lax.dot_general(k2_ref[0], q[:, PEER_N_KEYS:], _NT, preferred_element_type=F32)
    v1 = _top_values([s1[r:r + 8] for r in range(0, PEER_N_KEYS, 8)], K + 1)
    v2 = _top_values([s2[r:r + 8] for r in range(0, PEER_N_KEYS, 8)], K + 1)
    v1a, v1b = jnp.concatenate(v1[:8], axis=0), jnp.concatenate(v1[8:16], axis=0)
    v2a, v2b = jnp.concatenate(v2[:8], axis=0), jnp.concatenate(v2[8:16], axis=0)
    ninf = jnp.full_like(v1[0], -jnp.inf)
    cand = [v1[0] + v2a, v1[0] + v2b]
    cand += [v1[a] + v2a for a in range(1, 8)]
    cand += [v1b + v2[0]]
    cand += [jnp.concatenate([v1[K] + v2[0], v1[0] + v2[K]] + [ninf] * 6, axis=0)]
    top = _top_values(cand, K + 1)
    z = jnp.ones_like(top[0])
    for k in range(1, K):
        z = z + jnp.exp(top[k] - top[0])
    tau = 0.5 * (top[K - 1] + top[K])
    thr_ref[0] = jnp.exp((tau - v2[0]) - s1)
    e1_ref[0] = jnp.exp(s1 - v1[0]) * (1.0 / z)
    e2_ref[0] = jnp.exp(s2 - v2[0])


def _peer_route(q, keys1, keys2, *, tm=1024):
    T = q.shape[0]
    H = PEER_HEADS
    kspec = pl.BlockSpec((1, PEER_N_KEYS, PEER_N_KEYS), lambda i, h: (h, 0, 0))
    ospec = pl.BlockSpec((1, PEER_N_KEYS, tm), lambda i, h: (h, 0, i))
    oshape = jax.ShapeDtypeStruct((H, PEER_N_KEYS, T), F32)
    return pl.pallas_call(
        _peer_route_kernel,
        grid=(T // tm, H),
        in_specs=[pl.BlockSpec((tm, 2 * PEER_N_KEYS), lambda i, h: (i, h)), kspec, kspec],
        out_specs=[ospec, ospec, ospec],
        out_shape=[oshape, oshape, oshape],
        compiler_params=_params("parallel", "arbitrary"),
        name="peer_route",
    )(q, keys1, keys2)


def _peer_dense_kernel(xnt_ref, u_ref, v_ref, thr_c, thr_p, e1_c, e1_p, e2_c, e2_p,
                       res_ref, g_ref, o_ref, y_ref, ht0, ht1, wg0, wg1, wg0_prev,
                       *, tm, half, n_exp, n_blocks, n_sub):
    g = pl.program_id(0)
    d_model = y_ref.shape[1]
    kh = half // LANES
    prev = jnp.maximum(g - 1, 0)

    @pl.when(g == 0)
    def _():
        ht1[...] = jnp.zeros(ht1.shape, F32)
        wg0_prev[...] = jnp.zeros(wg0_prev.shape, BF16)

    @pl.when(prev % n_exp == 0)
    def _():
        y_ref[...] = jnp.zeros(y_ref.shape, F32)

    first_valid = (g >= 1).astype(F32)
    second_valid = (g <= n_blocks - 1).astype(F32)

    def pair_step(u_rows, ht_w, ht_r, wg_w, wg_r, thr_ref, e1_ref, e2_ref, key0, v_rows, valid):
        half_valid = 0.5 * valid
        ra, rc = half // n_sub, d_model // n_sub
        for j in range(n_sub):
            for il in range(j * ra // LANES, (j + 1) * ra // LANES):
                rs = slice(il * LANES, (il + 1) * LANES)
                for lt in range(tm // LANES):
                    cs = slice(lt * LANES, (lt + 1) * LANES)
                    gate = jnp.zeros((LANES, LANES), F32)
                    for h in range(PEER_HEADS):
                        e2v = e2_ref[h, :, cs]
                        thr = thr_ref[h, key0 + il:key0 + il + 1, cs]
                        e1 = e1_ref[h, key0 + il:key0 + il + 1, cs]
                        gate = gate + jnp.where(e2v >= thr, e2v, 0.0) * e1
                    hh = ht_r[rs, cs]
                    act = (half_valid * hh) * (1.0 + lax.erf(hh * (2.0 ** -0.5)))
                    wg_w[cs, rs] = (gate * act).T.astype(BF16)
            ht_w[j * ra:(j + 1) * ra, :] = jnp.dot(
                u_ref[u_rows + j * ra:u_rows + (j + 1) * ra, :], xnt_ref[...],
                preferred_element_type=F32)
            y_ref[:, j * rc:(j + 1) * rc] += jnp.dot(
                wg_r[...], v_ref[v_rows:v_rows + half, j * rc:(j + 1) * rc],
                preferred_element_type=F32)

    pair_step(0, ht0, ht1, wg1, wg0_prev, thr_p, e1_p, e2_p, kh, 0, first_valid)
    pair_step(half, ht1, ht0, wg0, wg1, thr_c, e1_c, e2_c, 0, half, second_valid)

    @pl.when(g < n_blocks)
    def _():
        wg0_prev[...] = wg0[...]

    @pl.when(jnp.logical_and(g >= 1, prev % n_exp == n_exp - 1))
    def _():
        h = res_ref[...] + y_ref[...]
        ms = jnp.mean(h * h, axis=-1, keepdims=True)
        o_ref[...] = h * lax.rsqrt(ms + RMS_EPS) * g_ref[...]


def _peer_dense(xnt, u, v, thr, e1, e2, res, g, *, tm=512, te=1024, n_sub=2):
    D, T = xnt.shape
    E = u.shape[0]
    H = PEER_HEADS
    kt = te // LANES
    n_exp = E // te
    n_blocks = (T // tm) * n_exp

    def cur(g):
        b = jnp.minimum(g, n_blocks - 1)
        return b // n_exp, b % n_exp

    def prev(g):
        b = jnp.maximum(g - 1, 0)
        return b // n_exp, b % n_exp

    once = dict(pipeline_mode=pl.Buffered(1))

    def key_spec(f):
        return pl.BlockSpec((H, kt, tm), lambda g: (0, f(g)[1], f(g)[0]))

    def tok_spec(f):
        return pl.BlockSpec((H, PEER_N_KEYS, tm), lambda g: (0, 0, f(g)[0]), **once)

    return pl.pallas_call(
        functools.partial(_peer_dense_kernel, tm=tm, half=te // 2, n_exp=n_exp,
                          n_blocks=n_blocks, n_sub=n_sub),
        grid=(n_blocks + 1,),
        in_specs=[pl.BlockSpec((D, tm), lambda g: (0, cur(g)[0]), **once),
                  pl.BlockSpec((te, D), lambda g: (cur(g)[1], 0)),
                  pl.BlockSpec((te, D), lambda g: (prev(g)[1], 0)),
                  key_spec(cur), key_spec(prev), key_spec(cur), key_spec(prev),
                  tok_spec(cur), tok_spec(prev),
                  pl.BlockSpec((tm, D), lambda g: (prev(g)[0], 0), **once),
                  pl.BlockSpec((1, D), lambda g: (0, 0), **once)],
        out_specs=pl.BlockSpec((tm, D), lambda g: (prev(g)[0], 0)),
        out_shape=jax.ShapeDtypeStruct((T, D), F32),
        scratch_shapes=[pltpu.VMEM((tm, D), F32),
                        pltpu.VMEM((te // 2, tm), F32), pltpu.VMEM((te // 2, tm), F32),
                        pltpu.VMEM((tm, te // 2), BF16), pltpu.VMEM((tm, te // 2), BF16),
                        pltpu.VMEM((tm, te // 2), BF16)],
        compiler_params=_params("arbitrary"),
        name="peer_dense",
    )(xnt, u, v, thr, thr, e1, e1, e2, e2, res, g.reshape(1, D))


def kernel(x, mem, norm_mix_g, w_in, conv_dw_w, conv_dw_b, conv_ln_g, conv_ln_b,
           lambda_q1, lambda_k1, lambda_q2, lambda_k2, diff_subln_g, w_out,
           norm_cross_g, norm_mem_g, w_cq, w_ckv, w_co,
           norm_peer_g, w_pq, peer_keys1, peer_keys2, peer_u, peer_v, final_norm_g):
    B, S, D = x.shape
    T = B * S
    mem_len = mem.shape[1]
    depth = w_in.shape[0]
    conv_w = conv_dw_w.shape[2]
    n_conv = 2 * conv_w
    qk_w = DIFF_HEADS * 2 * DIFF_HEAD_DIM

    assert depth == 1, "the final norm is fused into the PEER kernel of the only layer"
    h = x.reshape(T, D)
    for l in range(depth):
        lam_init = 0.8 - 0.6 * math.exp(-0.3 * l)
        q_scale = DIFF_HEAD_DIM ** -0.5 * math.log2(math.e)
        tn = 512
        zc = _norm_matmul(h, norm_mix_g[l], w_in[l], F32, tm=1024, tn=tn, n_cols=n_conv,
                          name="in_proj_conv")
        zqkv = _norm_matmul(h, norm_mix_g[l], w_in[l], BF16, tm=1024, tn=tn, col0=n_conv,
                            scaled_tiles=qk_w // tn, scale=q_scale, name="in_proj_qkv")
        conv_out = _conv_group(zc, conv_dw_w[l], conv_dw_b[l], conv_ln_g[l], conv_ln_b[l],
                               batch=B, seq=S)
        attn_out = _diff_attention(zqkv, lambda_q1[l], lambda_k1[l], lambda_q2[l], lambda_k2[l],
                                   diff_subln_g[l], batch=B, seq=S, lam_init=lam_init)
        h = _matmul_residual([(conv_out, w_out[l], 0), (attn_out, w_out[l], 1)], h,
                             tm=1024, tn=512, name="out_proj")
        kv = _norm_matmul(mem.reshape(B * mem_len, D), norm_mem_g[l], w_ckv[l], BF16,
                          tm=B * mem_len, tn=512, name="cross_kv_proj")
        cq = _norm_matmul(h, norm_cross_g[l], w_cq[l], BF16, tm=1024, tn=512, name="cross_q_proj")
        co = _cross_attention(cq, kv, batch=B, seq=S, mem_len=mem_len)
        h = _matmul_residual([(co, w_co[l], 0)], h, tm=1024, tn=512, name="cross_out_proj")
        pq, xnt = _norm_matmul(h, norm_peer_g[l], w_pq[l], BF16, tm=1024, tn=512,
                               emit_transposed=True, name="peer_q_proj")
        thr, e1, e2 = _peer_route(pq, peer_keys1[l].astype(BF16), peer_keys2[l].astype(BF16))
        h = _peer_dense(xnt, peer_u[l].astype(BF16), peer_v[l].astype(BF16),
                        thr, e1, e2, h, final_norm_g)
    return h.reshape(B, S, D)
```

```python
import functools
import math

import jax
import jax.numpy as jnp
from jax import lax
from jax.experimental import pallas as pl
from jax.experimental.pallas import tpu as pltpu

F32 = jnp.float32
BF16 = jnp.bfloat16

RMS_EPS = 1e-6
LN_EPS = 1e-5
NEG_INF = -1e30

LANES = 128
CONV_KERNEL = 31
CONV_HALO = 32
DIFF_HEADS = 8
DIFF_HEAD_DIM = 64
CROSS_HEADS = 4
PEER_HEADS = 8
PEER_N_KEYS = 128
PEER_TOPK = 16
VMEM_LIMIT = 56 * 1024 * 1024

_NT = (((1,), (1,)), ((), ()))


def _params(*sem):
    return pltpu.CompilerParams(dimension_semantics=sem, vmem_limit_bytes=VMEM_LIMIT)


def _norm_mm_kernel(x_ref, g_ref, w_ref, *out_refs, emit_transposed, scaled_cols, scale):
    o_ref, hn_ref = out_refs[0], out_refs[-1]

    @pl.when(pl.program_id(1) == 0)
    def _():
        x = x_ref[...]
        ms = jnp.mean(x * x, axis=-1, keepdims=True)
        hn = x * lax.rsqrt(ms + RMS_EPS) * g_ref[...]
        hn_ref[...] = hn.astype(BF16)
        if emit_transposed:
            out_refs[1][...] = hn.T.astype(BF16)

    acc = jnp.dot(hn_ref[...], w_ref[...].astype(BF16), preferred_element_type=F32)
    if scaled_cols:
        first = jnp.where(pl.program_id(1) == 0, scale, 1.0)
        o_ref[:, :scaled_cols] = (acc[:, :scaled_cols] * first).astype(o_ref.dtype)
        if scaled_cols < acc.shape[1]:
            o_ref[:, scaled_cols:] = acc[:, scaled_cols:].astype(o_ref.dtype)
    else:
        o_ref[...] = acc.astype(o_ref.dtype)


def _norm_matmul(x, g, w, out_dtype, *, tm, tn, name, col0=0, n_cols=None,
                 emit_transposed=False, scaled_cols=0, scale=1.0):
    M, K = x.shape
    n_cols = w.shape[1] - col0 if n_cols is None else n_cols
    assert col0 % tn == 0 and scaled_cols <= tn
    cb0 = col0 // tn
    w_mode = dict(pipeline_mode=pl.Buffered(1)) if n_cols == tn else {}
    out_specs = [pl.BlockSpec((tm, tn), lambda i, j: (i, j))]
    out_shape = [jax.ShapeDtypeStruct((M, n_cols), out_dtype)]
    if emit_transposed:
        out_specs.append(pl.BlockSpec((K, tm), lambda i, j: (0, i)))
        out_shape.append(jax.ShapeDtypeStruct((K, M), BF16))
    out = pl.pallas_call(
        functools.partial(_norm_mm_kernel, emit_transposed=emit_transposed,
                          scaled_cols=scaled_cols, scale=scale),
        grid=(M // tm, n_cols // tn),
        in_specs=[pl.BlockSpec((tm, K), lambda i, j: (i, 0)),
                  pl.BlockSpec((1, K), lambda i, j: (0, 0)),
                  pl.BlockSpec((K, tn), lambda i, j: (0, j + cb0), **w_mode)],
        out_specs=out_specs,
        out_shape=out_shape,
        scratch_shapes=[pltpu.VMEM((tm, K), BF16)],
        compiler_params=_params("parallel", "arbitrary"),
        name=name,
    )(x, g.reshape(1, K), w)
    return out if emit_transposed else out[0]


def _mm_res_kernel(*refs, n_pairs):
    res_ref, o_ref = refs[2 * n_pairs], refs[2 * n_pairs + 1]
    acc = res_ref[...]
    for p in range(n_pairs):
        acc = acc + jnp.dot(refs[2 * p][...], refs[2 * p + 1][...].astype(BF16),
                            preferred_element_type=F32)
    o_ref[...] = acc


def _matmul_residual(pairs, res, *, tm, tn, name):
    M, N = res.shape
    in_specs, args = [], []
    for a, w, rb in pairs:
        k = a.shape[1]
        in_specs += [pl.BlockSpec((tm, k), lambda i, j: (i, 0)),
                     pl.BlockSpec((k, tn), lambda i, j, rb=rb: (rb, j))]
        args += [a, w]
    in_specs.append(pl.BlockSpec((tm, tn), lambda i, j: (i, j)))
    return pl.pallas_call(
        functools.partial(_mm_res_kernel, n_pairs=len(pairs)),
        grid=(M // tm, N // tn),
        in_specs=in_specs,
        out_specs=pl.BlockSpec((tm, tn), lambda i, j: (i, j)),
        out_shape=jax.ShapeDtypeStruct((M, N), F32),
        compiler_params=_params("parallel", "arbitrary"),
        name=name,
    )(*args, res)


def _conv_kernel(a_ref, gt_ref, ah_ref, gh_ref, w_ref, b_ref, lg_ref, lb_ref, o_ref, ubuf,
                 *, ts, rows):
    i = pl.program_id(2)
    uh = ah_ref[...] * jax.nn.sigmoid(gh_ref[...])
    ubuf[0:CONV_HALO, :] = jnp.where(i > 0, uh, 0.0)
    ubuf[CONV_HALO:, :] = a_ref[...] * jax.nn.sigmoid(gt_ref[...])
    bias = b_ref[...]
    lg = lg_ref[...]
    lb = lb_ref[...]
    first = CONV_HALO - (CONV_KERNEL - 1)
    for r in range(ts // rows):
        acc = jnp.zeros((rows, LANES), F32) + bias
        for j in range(CONV_KERNEL):
            start = r * rows + first + j
            acc = acc + w_ref[j:j + 1, :] * ubuf[start:start + rows, :]
        mu = jnp.mean(acc, axis=-1, keepdims=True)
        d = acc - mu
        var = jnp.mean(d * d, axis=-1, keepdims=True)
        y = d * lax.rsqrt(var + LN_EPS) * lg + lb
        o_ref[r * rows:(r + 1) * rows, :] = (y * jax.nn.sigmoid(y)).astype(o_ref.dtype)


def _conv_group(zc, w_dw, b_dw, ln_g, ln_b, *, batch, seq, ts=512, rows=128):
    T, C2 = zc.shape
    C = C2 // 2
    G = C // LANES
    ns = seq // ts
    hb = ts // CONV_HALO

    def main(off):
        return pl.BlockSpec((ts, LANES), lambda b, g, i: (b * ns + i, g + off))

    def halo(off):
        return pl.BlockSpec(
            (CONV_HALO, LANES),
            lambda b, g, i: (jnp.maximum((b * ns + i) * hb - 1, 0), g + off))

    vec = pl.BlockSpec((1, LANES), lambda b, g, i: (0, g))
    return pl.pallas_call(
        functools.partial(_conv_kernel, ts=ts, rows=rows),
        grid=(batch, G, ns),
        in_specs=[main(0), main(G), halo(0), halo(G),
                  pl.BlockSpec((CONV_KERNEL, LANES), lambda b, g, i: (0, g)),
                  vec, vec, vec],
        out_specs=pl.BlockSpec((ts, LANES), lambda b, g, i: (b * ns + i, g)),
        out_shape=jax.ShapeDtypeStruct((T, C), BF16),
        scratch_shapes=[pltpu.VMEM((ts + CONV_HALO, LANES), F32)],
        compiler_params=_params("parallel", "parallel", "arbitrary"),
        name="conv_group",
    )(zc, zc, zc, zc, w_dw, b_dw.reshape(1, C), ln_g.reshape(1, C), ln_b.reshape(1, C))


def _diff_attn_kernel(q_ref, k_ref, v_ref, lq1, lk1, lq2, lk2, sg_ref, o_ref,
                      qs_ref, vx_ref, m_ref, acc_ref, *, tq, rows, lam_init):
    qi = pl.program_id(2)
    per = tq // rows
    n_groups = 2 * per

    @pl.when(qi == 0)
    def _():
        vx_ref[:, :LANES] = v_ref[...]
        vx_ref[:, LANES:] = jnp.ones((vx_ref.shape[0], LANES), BF16)

    q = q_ref[...]
    lane = lax.broadcasted_iota(jnp.int32, (tq, LANES), 1)
    zero = jnp.zeros_like(q)
    for c, qc in enumerate((jnp.where(lane < DIFF_HEAD_DIM, q, zero),
                            jnp.where(lane >= DIFF_HEAD_DIM, q, zero))):
        for r in range(per):
            qs_ref[c * per + r] = qc[r * rows:(r + 1) * rows]
    m_ref[...] = jnp.full(m_ref.shape, NEG_INF, F32)
    acc_ref[...] = jnp.zeros(acc_ref.shape, F32)

    def step(ki, masked):
        start = pl.multiple_of(ki * tq, tq)
        kb = k_ref[pl.ds(start, tq), :]
        vb = vx_ref[pl.ds(start, tq), :]
        for g in range(n_groups):
            s = lax.dot_general(qs_ref[g], kb, _NT, preferred_element_type=F32)
            if masked:
                row = lax.broadcasted_iota(jnp.int32, (rows, tq), 0) + (g % per) * rows
                col = lax.broadcasted_iota(jnp.int32, (rows, tq), 1)
                s = jnp.where(col <= row, s, NEG_INF)
            m_old = m_ref[g]
            m_new = jnp.maximum(m_old, jnp.max(s, axis=-1, keepdims=True))
            alpha = jnp.exp2(m_old - m_new)
            p = jnp.exp2(s - jnp.tile(m_new, (1, tq // LANES)))
            acc_ref[g] = (jnp.tile(alpha, (1, 2)) * acc_ref[g]
                          + jnp.dot(p.astype(BF16), vb, preferred_element_type=F32))
            m_ref[g] = m_new

    def body(k2, c):
        step(2 * k2, False)
        step(2 * k2 + 1, False)
        return c

    lax.fori_loop(0, qi // 2, body, 0)

    @pl.when(qi % 2 == 1)
    def _():
        step(qi - 1, False)

    step(qi, True)

    lam = (jnp.exp(jnp.sum(lq1[...] * lk1[...], axis=-1, keepdims=True))
           - jnp.exp(jnp.sum(lq2[...] * lk2[...], axis=-1, keepdims=True)) + lam_init)
    for r in range(per):
        a1, a2 = acc_ref[r], acc_ref[per + r]
        of = a1[:, :LANES] / a1[:, LANES:] - lam * (a2[:, :LANES] / a2[:, LANES:])
        of = of * lax.rsqrt(jnp.mean(of * of, axis=-1, keepdims=True) + RMS_EPS)
        of = of * sg_ref[...] * (1.0 - lam_init)
        o_ref[r * rows:(r + 1) * rows, :] = of.astype(o_ref.dtype)


def _diff_attention(zqkv, lq1, lk1, lq2, lk2, subln_g, *, batch, seq, lam_init, tq=512, rows=128):
    T = zqkv.shape[0]
    H = DIFF_HEADS
    nq = seq // tq
    n_groups = 2 * tq // rows
    lam_spec = pl.BlockSpec((1, DIFF_HEAD_DIM), lambda b, h, i: (0, 0))
    return pl.pallas_call(
        functools.partial(_diff_attn_kernel, tq=tq, rows=rows, lam_init=lam_init),
        grid=(batch, H, nq),
        in_specs=[pl.BlockSpec((tq, LANES), lambda b, h, i: (b * nq + i, h)),
                  pl.BlockSpec((seq, LANES), lambda b, h, i: (b, H + h)),
                  pl.BlockSpec((seq, LANES), lambda b, h, i: (b, 2 * H + h)),
                  lam_spec, lam_spec, lam_spec, lam_spec,
                  pl.BlockSpec((1, LANES), lambda b, h, i: (0, 0))],
        out_specs=pl.BlockSpec((tq, LANES), lambda b, h, i: (b * nq + i, h)),
        out_shape=jax.ShapeDtypeStruct((T, H * LANES), BF16),
        scratch_shapes=[pltpu.VMEM((n_groups, rows, LANES), BF16),
                        pltpu.VMEM((seq, 2 * LANES), BF16),
                        pltpu.VMEM((n_groups, rows, LANES), F32),
                        pltpu.VMEM((n_groups, rows, 2 * LANES), F32)],
        compiler_params=_params("parallel", "parallel", "arbitrary"),
        name="diff_attention",
    )(zqkv, zqkv, zqkv,
      lq1.reshape(1, -1), lk1.reshape(1, -1), lq2.reshape(1, -1), lk2.reshape(1, -1),
      subln_g.reshape(1, -1))


def _cross_attn_kernel(q_ref, k_ref, v_ref, o_ref, *, dh, scale):
    for h in range(CROSS_HEADS):
        sl = slice(h * dh, (h + 1) * dh)
        s = lax.dot_general(q_ref[:, sl], k_ref[:, sl], _NT,
                            preferred_element_type=F32) * scale
        p = jnp.exp(s - jnp.max(s, axis=-1, keepdims=True))
        l = jnp.sum(p, axis=-1, keepdims=True)
        o = jnp.dot(p.astype(BF16), v_ref[:, sl], preferred_element_type=F32)
        o_ref[:, sl] = (o / l).astype(o_ref.dtype)


def _cross_attention(q, kv, *, batch, seq, mem_len, tq=512):
    T, D = q.shape
    dh = D // CROSS_HEADS
    nq = seq // tq
    return pl.pallas_call(
        functools.partial(_cross_attn_kernel, dh=dh, scale=dh ** -0.5),
        grid=(batch, nq),
        in_specs=[pl.BlockSpec((tq, D), lambda b, i: (b * nq + i, 0)),
                  pl.BlockSpec((mem_len, D), lambda b, i: (b, 0)),
                  pl.BlockSpec((mem_len, D), lambda b, i: (b, 1))],
        out_specs=pl.BlockSpec((tq, D), lambda b, i: (b * nq + i, 0)),
        out_shape=jax.ShapeDtypeStruct((T, D), BF16),
        compiler_params=_params("parallel", "arbitrary"),
        name="cross_attention",
    )(q, kv, kv)


def _merge_exchange_pairs(n):
    t = max(1, (n - 1).bit_length())
    pairs = []
    p = 1 << (t - 1)
    while p > 0:
        q, r, d = 1 << (t - 1), 0, p
        while d > 0:
            pairs += [(i, i + d) for i in range(n - d) if (i & p) == r]
            d, q, r = q - p, q >> 1, p
        p >>= 1
    return pairs


def _top_values(rows, n):
    s = list(rows)
    for i, j in _merge_exchange_pairs(len(s)):
        s[i], s[j] = jnp.maximum(s[i], s[j]), jnp.minimum(s[i], s[j])
    ninf = jnp.full_like(s[0], -jnp.inf)
    vals = []
    for k in range(n):
        m = jnp.max(s[0], axis=0, keepdims=True)
        vals.append(m)
        hit = s[0] == m
        for r in range(min(len(s), n - k - 1)):
            s[r] = jnp.where(hit, s[r + 1] if r + 1 < len(s) else ninf, s[r])
    return vals


def _peer_route_kernel(q_ref, k1_ref, k2_ref, thr_ref, e1_ref, e2_ref):
    K = PEER_TOPK
    q = q_ref[...]
    s1 = lax.dot_general(k1_ref[0], q[:, :PEER_N_KEYS], _NT, preferred_element_type=F32)
    s2 = lax.dot_general(k2_ref[0], q[:, PEER_N_KEYS:], _NT, preferred_element_type=F32)
    v1 = _top_values([s1[r:r + 8] for r in range(0, PEER_N_KEYS, 8)], K + 1)
    v2 = _top_values([s2[r:r + 8] for r in range(0, PEER_N_KEYS, 8)], K + 1)
    v1a, v1b = jnp.concatenate(v1[:8], axis=0), jnp.concatenate(v1[8:16], axis=0)
    v2a, v2b = jnp.concatenate(v2[:8], axis=0), jnp.concatenate(v2[8:16], axis=0)
    ninf = jnp.full_like(v1[0], -jnp.inf)
    cand = [v1[0] + v2a, v1[0] + v2b]
    cand += [v1[a] + v2a for a in range(1, 8)]
    cand += [v1b + v2[0]]
    cand += [jnp.concatenate([v1[K] + v2[0], v1[0] + v2[K]] + [ninf] * 6, axis=0)]
    top = _top_values(cand, K + 1)
    z = jnp.ones_like(top[0])
    for k in range(1, K):
        z = z + jnp.exp(top[k] - top[0])
    tau = 0.5 * (top[K - 1] + top[K])
    thr_ref[0] = jnp.exp((tau - v2[0]) - s1)
    e1_ref[0] = jnp.exp(s1 - v1[0]) * (1.0 / z)
    e2_ref[0] = jnp.exp(s2 - v2[0])


def _peer_route(q, keys1, keys2, *, tm=1024):
    T = q.shape[0]
    H = PEER_HEADS
    kspec = pl.BlockSpec((1, PEER_N_KEYS, PEER_N_KEYS), lambda i, h: (h, 0, 0))
    ospec = pl.BlockSpec((1, PEER_N_KEYS, tm), lambda i, h: (h, 0, i))
    oshape = jax.ShapeDtypeStruct((H, PEER_N_KEYS, T), F32)
    return pl.pallas_call(
        _peer_route_kernel,
        grid=(T // tm, H),
        in_specs=[pl.BlockSpec((tm, 2 * PEER_N_KEYS), lambda i, h: (i, h)), kspec, kspec],
        out_specs=[ospec, ospec, ospec],
        out_shape=[oshape, oshape, oshape],
        compiler_params=_params("parallel", "arbitrary"),
        name="peer_route",
    )(q, keys1, keys2)


def _peer_dense_kernel(xnt_ref, u_ref, v_ref, thr_c, thr_p, e1_c, e1_p, e2_c, e2_p,
                       res_ref, g_ref, o_ref, y_ref, ht0, ht1, wg0, wg1, wg0_prev,
                       *, tm, half, n_exp, n_blocks, n_sub):
    g = pl.program_id(0)
    d_model = y_ref.shape[1]
    kh = half // LANES
    prev = jnp.maximum(g - 1, 0)

    @pl.when(g == 0)
    def _():
        ht1[...] = jnp.zeros(ht1.shape, F32)
        wg0_prev[...] = jnp.zeros(wg0_prev.shape, BF16)

    @pl.when(prev % n_exp == 0)
    def _():
        y_ref[...] = jnp.zeros(y_ref.shape, F32)

    first_valid = (g >= 1).astype(F32)
    second_valid = (g <= n_blocks - 1).astype(F32)

    def pair_step(u_rows, ht_w, ht_r, wg_w, wg_r, thr_ref, e1_ref, e2_ref, key0, v_rows, valid):
        half_valid = 0.5 * valid
        ra, rc = half // n_sub, d_model // n_sub
        for j in range(n_sub):
            for il in range(j * ra // LANES, (j + 1) * ra // LANES):
                rs = slice(il * LANES, (il + 1) * LANES)
                for lt in range(tm // LANES):
                    cs = slice(lt * LANES, (lt + 1) * LANES)
                    gate = jnp.zeros((LANES, LANES), F32)
                    for h in range(PEER_HEADS):
                        e2v = e2_ref[h, :, cs]
                        thr = thr_ref[h, key0 + il:key0 + il + 1, cs]
                        e1 = e1_ref[h, key0 + il:key0 + il + 1, cs]
                        gate = gate + jnp.where(e2v >= thr, e2v, 0.0) * e1
                    hh = ht_r[rs, cs]
                    act = (half_valid * hh) * (1.0 + lax.erf(hh * (2.0 ** -0.5)))
                    wg_w[cs, rs] = (gate * act).T.astype(BF16)
            ht_w[j * ra:(j + 1) * ra, :] = jnp.dot(
                u_ref[u_rows + j * ra:u_rows + (j + 1) * ra, :], xnt_ref[...],
                preferred_element_type=F32)
            y_ref[:, j * rc:(j + 1) * rc] += jnp.dot(
                wg_r[...], v_ref[v_rows:v_rows + half, j * rc:(j + 1) * rc],
                preferred_element_type=F32)

    pair_step(0, ht0, ht1, wg1, wg0_prev, thr_p, e1_p, e2_p, kh, 0, first_valid)
    pair_step(half, ht1, ht0, wg0, wg1, thr_c, e1_c, e2_c, 0, half, second_valid)

    @pl.when(g < n_blocks)
    def _():
        wg0_prev[...] = wg0[...]

    @pl.when(jnp.logical_and(g >= 1, prev % n_exp == n_exp - 1))
    def _():
        h = res_ref[...] + y_ref[...]
        ms = jnp.mean(h * h, axis=-1, keepdims=True)
        o_ref[...] = h * lax.rsqrt(ms + RMS_EPS) * g_ref[...]


def _peer_dense(xnt, u, v, thr, e1, e2, res, g, *, tm=512, te=1024, n_sub=2):
    D, T = xnt.shape
    E = u.shape[0]
    H = PEER_HEADS
    kt = te // LANES
    n_exp = E // te
    n_blocks = (T // tm) * n_exp

    def cur(g):
        b = jnp.minimum(g, n_blocks - 1)
        return b // n_exp, b % n_exp

    def prev(g):
        b = jnp.maximum(g - 1, 0)
        return b // n_exp, b % n_exp

    once = dict(pipeline_mode=pl.Buffered(1))

    def key_spec(f):
        return pl.BlockSpec((H, kt, tm), lambda g: (0, f(g)[1], f(g)[0]))

    def tok_spec(f):
        return pl.BlockSpec((H, PEER_N_KEYS, tm), lambda g: (0, 0, f(g)[0]), **once)

    return pl.pallas_call(
        functools.partial(_peer_dense_kernel, tm=tm, half=te // 2, n_exp=n_exp,
                          n_blocks=n_blocks, n_sub=n_sub),
        grid=(n_blocks + 1,),
        in_specs=[pl.BlockSpec((D, tm), lambda g: (0, cur(g)[0]), **once),
                  pl.BlockSpec((te, D), lambda g: (cur(g)[1], 0)),
                  pl.BlockSpec((te, D), lambda g: (prev(g)[1], 0)),
                  key_spec(cur), key_spec(prev), key_spec(cur), key_spec(prev),
                  tok_spec(cur), tok_spec(prev),
                  pl.BlockSpec((tm, D), lambda g: (prev(g)[0], 0), **once),
                  pl.BlockSpec((1, D), lambda g: (0, 0), **once)],
        out_specs=pl.BlockSpec((tm, D), lambda g: (prev(g)[0], 0)),
        out_shape=jax.ShapeDtypeStruct((T, D), F32),
        scratch_shapes=[pltpu.VMEM((tm, D), F32),
                        pltpu.VMEM((te // 2, tm), F32), pltpu.VMEM((te // 2, tm), F32),
                        pltpu.VMEM((tm, te // 2), BF16), pltpu.VMEM((tm, te // 2), BF16),
                        pltpu.VMEM((tm, te // 2), BF16)],
        compiler_params=_params("arbitrary"),
        name="peer_dense",
    )(xnt, u, v, thr, thr, e1, e1, e2, e2, res, g.reshape(1, D))


def kernel(x, mem, norm_mix_g, w_in, conv_dw_w, conv_dw_b, conv_ln_g, conv_ln_b,
           lambda_q1, lambda_k1, lambda_q2, lambda_k2, diff_subln_g, w_out,
           norm_cross_g, norm_mem_g, w_cq, w_ckv, w_co,
           norm_peer_g, w_pq, peer_keys1, peer_keys2, peer_u, peer_v, final_norm_g):
    B, S, D = x.shape
    T = B * S
    mem_len = mem.shape[1]
    depth = w_in.shape[0]
    conv_w = conv_dw_w.shape[2]
    n_conv = 2 * conv_w
    qk_w = DIFF_HEADS * 2 * DIFF_HEAD_DIM

    assert depth == 1, "the final norm is fused into the PEER kernel of the only layer"
    h = x.reshape(T, D)
    for l in range(depth):
        lam_init = 0.8 - 0.6 * math.exp(-0.3 * l)
        q_scale = DIFF_HEAD_DIM ** -0.5 * math.log2(math.e)
        zc = _norm_matmul(h, norm_mix_g[l], w_in[l], F32, tm=512, tn=n_conv, n_cols=n_conv,
                          name="in_proj_conv")
        zqkv = _norm_matmul(h, norm_mix_g[l], w_in[l], BF16, tm=1024, tn=qk_w, col0=n_conv,
                            scaled_cols=qk_w, scale=q_scale, name="in_proj_qkv")
        conv_out = _conv_group(zc, conv_dw_w[l], conv_dw_b[l], conv_ln_g[l], conv_ln_b[l],
                               batch=B, seq=S)
        attn_out = _diff_attention(zqkv, lambda_q1[l], lambda_k1[l], lambda_q2[l], lambda_k2[l],
                                   diff_subln_g[l], batch=B, seq=S, lam_init=lam_init)
        h = _matmul_residual([(conv_out, w_out[l], 0), (attn_out, w_out[l], 1)], h,
                             tm=1024, tn=512, name="out_proj")
        kv = _norm_matmul(mem.reshape(B * mem_len, D), norm_mem_g[l], w_ckv[l], BF16,
                          tm=B * mem_len, tn=512, name="cross_kv_proj")
        cq = _norm_matmul(h, norm_cross_g[l], w_cq[l], BF16, tm=512, tn=D, name="cross_q_proj")
        co = _cross_attention(cq, kv, batch=B, seq=S, mem_len=mem_len)
        h = _matmul_residual([(co, w_co[l], 0)], h, tm=1024, tn=512, name="cross_out_proj")
        pq, xnt = _norm_matmul(h, norm_peer_g[l], w_pq[l], BF16, tm=512, tn=w_pq.shape[2],
                               emit_transposed=True, name="peer_q_proj")
        thr, e1, e2 = _peer_route(pq, peer_keys1[l].astype(BF16), peer_keys2[l].astype(BF16))
        h = _peer_dense(xnt, peer_u[l].astype(BF16), peer_v[l].astype(BF16),
                        thr, e1, e2, h, final_norm_g)
    return h.reshape(B, S, D)
```

```python
import functools
import math

import jax
import jax.numpy as jnp
from jax import lax
from jax.experimental import pallas as pl
from jax.experimental.pallas import tpu as pltpu

F32 = jnp.float32
BF16 = jnp.bfloat16

RMS_EPS = 1e-6
LN_EPS = 1e-5
NEG_INF = -1e30

LANES = 128
CONV_KERNEL = 31
CONV_HALO = 32
DIFF_HEADS = 8
DIFF_HEAD_DIM = 64
CROSS_HEADS = 4
PEER_HEADS = 8
PEER_N_KEYS = 128
PEER_TOPK = 16
VMEM_LIMIT = 56 * 1024 * 1024

_NT = (((1,), (1,)), ((), ()))


def _params(*sem):
    return pltpu.CompilerParams(dimension_semantics=sem, vmem_limit_bytes=VMEM_LIMIT)


def _bf16_weights(w_ref, cache_ref):
    if cache_ref is None:
        return w_ref[...].astype(BF16)

    @pl.when(jnp.logical_and(pl.program_id(0) == 0, pl.program_id(1) == 0))
    def _():
        cache_ref[...] = w_ref[...].astype(BF16)

    return cache_ref[...]


def _norm_mm_kernel(x_ref, g_ref, w_ref, *rest, emit_transposed, resident, scaled_cols, scale):
    n_out = 2 if emit_transposed else 1
    o_ref, hn_ref = rest[0], rest[n_out]
    wb_ref = rest[n_out + 1] if resident else None

    @pl.when(pl.program_id(1) == 0)
    def _():
        x = x_ref[...]
        ms = jnp.mean(x * x, axis=-1, keepdims=True)
        hn = x * lax.rsqrt(ms + RMS_EPS) * g_ref[...]
        hn_ref[...] = hn.astype(BF16)
        if emit_transposed:
            rest[1][...] = hn.T.astype(BF16)

    acc = jnp.dot(hn_ref[...], _bf16_weights(w_ref, wb_ref), preferred_element_type=F32)
    if scaled_cols:
        first = jnp.where(pl.program_id(1) == 0, scale, 1.0)
        o_ref[:, :scaled_cols] = (acc[:, :scaled_cols] * first).astype(o_ref.dtype)
        if scaled_cols < acc.shape[1]:
            o_ref[:, scaled_cols:] = acc[:, scaled_cols:].astype(o_ref.dtype)
    else:
        o_ref[...] = acc.astype(o_ref.dtype)


def _norm_matmul(x, g, w, out_dtype, *, tm, tn, name, col0=0, n_cols=None,
                 emit_transposed=False, scaled_cols=0, scale=1.0):
    M, K = x.shape
    n_cols = w.shape[1] - col0 if n_cols is None else n_cols
    assert col0 % tn == 0 and scaled_cols <= tn
    cb0 = col0 // tn
    resident = n_cols == tn
    w_mode = dict(pipeline_mode=pl.Buffered(1)) if resident else {}
    out_specs = [pl.BlockSpec((tm, tn), lambda i, j: (i, j))]
    out_shape = [jax.ShapeDtypeStruct((M, n_cols), out_dtype)]
    if emit_transposed:
        out_specs.append(pl.BlockSpec((K, tm), lambda i, j: (0, i)))
        out_shape.append(jax.ShapeDtypeStruct((K, M), BF16))
    scratch = [pltpu.VMEM((tm, K), BF16)] + ([pltpu.VMEM((K, tn), BF16)] if resident else [])
    out = pl.pallas_call(
        functools.partial(_norm_mm_kernel, emit_transposed=emit_transposed, resident=resident,
                          scaled_cols=scaled_cols, scale=scale),
        grid=(M // tm, n_cols // tn),
        in_specs=[pl.BlockSpec((tm, K), lambda i, j: (i, 0)),
                  pl.BlockSpec((1, K), lambda i, j: (0, 0)),
                  pl.BlockSpec((K, tn), lambda i, j: (0, j + cb0), **w_mode)],
        out_specs=out_specs,
        out_shape=out_shape,
        scratch_shapes=scratch,
        compiler_params=_params("arbitrary" if resident else "parallel", "arbitrary"),
        name=name,
    )(x, g.reshape(1, K), w)
    return out if emit_transposed else out[0]


def _mm_res_kernel(*refs, n_pairs, resident):
    res_ref, o_ref = refs[2 * n_pairs], refs[2 * n_pairs + 1]
    caches = refs[2 * n_pairs + 2:] if resident else (None,) * n_pairs
    acc = res_ref[...]
    for p in range(n_pairs):
        acc = acc + jnp.dot(refs[2 * p][...], _bf16_weights(refs[2 * p + 1], caches[p]),
                            preferred_element_type=F32)
    o_ref[...] = acc


def _matmul_residual(pairs, res, *, tm, tn, name):
    M, N = res.shape
    resident = tn == N
    w_mode = dict(pipeline_mode=pl.Buffered(1)) if resident else {}
    in_specs, args, scratch = [], [], []
    for a, w, rb in pairs:
        k = a.shape[1]
        in_specs += [pl.BlockSpec((tm, k), lambda i, j: (i, 0)),
                     pl.BlockSpec((k, tn), lambda i, j, rb=rb: (rb, j), **w_mode)]
        args += [a, w]
        if resident:
            scratch.append(pltpu.VMEM((k, tn), BF16))
    in_specs.append(pl.BlockSpec((tm, tn), lambda i, j: (i, j)))
    return pl.pallas_call(
        functools.partial(_mm_res_kernel, n_pairs=len(pairs), resident=resident),
        grid=(M // tm, N // tn),
        in_specs=in_specs,
        out_specs=pl.BlockSpec((tm, tn), lambda i, j: (i, j)),
        out_shape=jax.ShapeDtypeStruct((M, N), F32),
        scratch_shapes=scratch,
        compiler_params=_params("arbitrary" if resident else "parallel", "arbitrary"),
        name=name,
    )(*args, res)


def _conv_kernel(a_ref, gt_ref, ah_ref, gh_ref, w_ref, b_ref, lg_ref, lb_ref, o_ref, ubuf,
                 *, ts, rows):
    i = pl.program_id(2)
    uh = ah_ref[...] * jax.nn.sigmoid(gh_ref[...])
    ubuf[0:CONV_HALO, :] = jnp.where(i > 0, uh, 0.0)
    ubuf[CONV_HALO:, :] = a_ref[...] * jax.nn.sigmoid(gt_ref[...])
    bias = b_ref[...]
    lg = lg_ref[...]
    lb = lb_ref[...]
    first = CONV_HALO - (CONV_KERNEL - 1)
    for r in range(ts // rows):
        acc = jnp.zeros((rows, LANES), F32) + bias
        for j in range(CONV_KERNEL):
            start = r * rows + first + j
            acc = acc + w_ref[j:j + 1, :] * ubuf[start:start + rows, :]
        mu = jnp.mean(acc, axis=-1, keepdims=True)
        d = acc - mu
        var = jnp.mean(d * d, axis=-1, keepdims=True)
        y = d * lax.rsqrt(var + LN_EPS) * lg + lb
        o_ref[r * rows:(r + 1) * rows, :] = (y * jax.nn.sigmoid(y)).astype(o_ref.dtype)


def _conv_group(zc, w_dw, b_dw, ln_g, ln_b, *, batch, seq, ts=512, rows=128):
    T, C2 = zc.shape
    C = C2 // 2
    G = C // LANES
    ns = seq // ts
    hb = ts // CONV_HALO

    def main(off):
        return pl.BlockSpec((ts, LANES), lambda b, g, i: (b * ns + i, g + off))

    def halo(off):
        return pl.BlockSpec(
            (CONV_HALO, LANES),
            lambda b, g, i: (jnp.maximum((b * ns + i) * hb - 1, 0), g + off))

    vec = pl.BlockSpec((1, LANES), lambda b, g, i: (0, g))
    return pl.pallas_call(
        functools.partial(_conv_kernel, ts=ts, rows=rows),
        grid=(batch, G, ns),
        in_specs=[main(0), main(G), halo(0), halo(G),
                  pl.BlockSpec((CONV_KERNEL, LANES), lambda b, g, i: (0, g)),
                  vec, vec, vec],
        out_specs=pl.BlockSpec((ts, LANES), lambda b, g, i: (b * ns + i, g)),
        out_shape=jax.ShapeDtypeStruct((T, C), BF16),
        scratch_shapes=[pltpu.VMEM((ts + CONV_HALO, LANES), F32)],
        compiler_params=_params("parallel", "parallel", "arbitrary"),
        name="conv_group",
    )(zc, zc, zc, zc, w_dw, b_dw.reshape(1, C), ln_g.reshape(1, C), ln_b.reshape(1, C))


def _diff_attn_kernel(q_ref, k_ref, v_ref, lq1, lk1, lq2, lk2, sg_ref, o_ref,
                      qs_ref, vx_ref, m_ref, acc_ref, *, tq, rows, lam_init):
    qi = pl.program_id(2)
    per = tq // rows
    n_groups = 2 * per

    @pl.when(qi == 0)
    def _():
        vx_ref[:, :LANES] = v_ref[...]
        vx_ref[:, LANES:] = jnp.ones((vx_ref.shape[0], LANES), BF16)

    q = q_ref[...]
    lane = lax.broadcasted_iota(jnp.int32, (tq, LANES), 1)
    zero = jnp.zeros_like(q)
    for c, qc in enumerate((jnp.where(lane < DIFF_HEAD_DIM, q, zero),
                            jnp.where(lane >= DIFF_HEAD_DIM, q, zero))):
        for r in range(per):
            qs_ref[c * per + r] = qc[r * rows:(r + 1) * rows]
    m_ref[...] = jnp.full(m_ref.shape, NEG_INF, F32)
    acc_ref[...] = jnp.zeros(acc_ref.shape, F32)

    def step(ki, masked):
        start = pl.multiple_of(ki * tq, tq)
        kb = k_ref[pl.ds(start, tq), :]
        vb = vx_ref[pl.ds(start, tq), :]
        for g in range(n_groups):
            s = lax.dot_general(qs_ref[g], kb, _NT, preferred_element_type=F32)
            if masked:
                row = lax.broadcasted_iota(jnp.int32, (rows, tq), 0) + (g % per) * rows
                col = lax.broadcasted_iota(jnp.int32, (rows, tq), 1)
                s = jnp.where(col <= row, s, NEG_INF)
            m_old = m_ref[g]
            m_new = jnp.maximum(m_old, jnp.max(s, axis=-1, keepdims=True))
            alpha = jnp.exp2(m_old - m_new)
            p = jnp.exp2(s - jnp.tile(m_new, (1, tq // LANES)))
            acc_ref[g] = (jnp.tile(alpha, (1, 2)) * acc_ref[g]
                          + jnp.dot(p.astype(BF16), vb, preferred_element_type=F32))
            m_ref[g] = m_new

    def body(k2, c):
        step(2 * k2, False)
        step(2 * k2 + 1, False)
        return c

    lax.fori_loop(0, qi // 2, body, 0)

    @pl.when(qi % 2 == 1)
    def _():
        step(qi - 1, False)

    step(qi, True)

    lam = (jnp.exp(jnp.sum(lq1[...] * lk1[...], axis=-1, keepdims=True))
           - jnp.exp(jnp.sum(lq2[...] * lk2[...], axis=-1, keepdims=True)) + lam_init)
    for r in range(per):
        a1, a2 = acc_ref[r], acc_ref[per + r]
        of = a1[:, :LANES] / a1[:, LANES:] - lam * (a2[:, :LANES] / a2[:, LANES:])
        of = of * lax.rsqrt(jnp.mean(of * of, axis=-1, keepdims=True) + RMS_EPS)
        of = of * sg_ref[...] * (1.0 - lam_init)
        o_ref[r * rows:(r + 1) * rows, :] = of.astype(o_ref.dtype)


def _diff_attention(zqk, zv, lq1, lk1, lq2, lk2, subln_g, *, batch, seq, lam_init,
                    tq=512, rows=128):
    T = zqk.shape[0]
    H = DIFF_HEADS
    nq = seq // tq
    n_groups = 2 * tq // rows
    lam_spec = pl.BlockSpec((1, DIFF_HEAD_DIM), lambda b, h, i: (0, 0))
    return pl.pallas_call(
        functools.partial(_diff_attn_kernel, tq=tq, rows=rows, lam_init=lam_init),
        grid=(batch, H, nq),
        in_specs=[pl.BlockSpec((tq, LANES), lambda b, h, i: (b * nq + i, h)),
                  pl.BlockSpec((seq, LANES), lambda b, h, i: (b, H + h)),
                  pl.BlockSpec((seq, LANES), lambda b, h, i: (b, h)),
                  lam_spec, lam_spec, lam_spec, lam_spec,
                  pl.BlockSpec((1, LANES), lambda b, h, i: (0, 0))],
        out_specs=pl.BlockSpec((tq, LANES), lambda b, h, i: (b * nq + i, h)),
        out_shape=jax.ShapeDtypeStruct((T, H * LANES), BF16),
        scratch_shapes=[pltpu.VMEM((n_groups, rows, LANES), BF16),
                        pltpu.VMEM((seq, 2 * LANES), BF16),
                        pltpu.VMEM((n_groups, rows, LANES), F32),
                        pltpu.VMEM((n_groups, rows, 2 * LANES), F32)],
        compiler_params=_params("parallel", "parallel", "arbitrary"),
        name="diff_attention",
    )(zqk, zqk, zv,
      lq1.reshape(1, -1), lk1.reshape(1, -1), lq2.reshape(1, -1), lk2.reshape(1, -1),
      subln_g.reshape(1, -1))


def _cross_attn_kernel(q_ref, k_ref, v_ref, o_ref, *, dh, scale):
    for h in range(CROSS_HEADS):
        sl = slice(h * dh, (h + 1) * dh)
        s = lax.dot_general(q_ref[:, sl], k_ref[:, sl], _NT,
                            preferred_element_type=F32) * scale
        p = jnp.exp(s - jnp.max(s, axis=-1, keepdims=True))
        l = jnp.sum(p, axis=-1, keepdims=True)
        o = jnp.dot(p.astype(BF16), v_ref[:, sl], preferred_element_type=F32)
        o_ref[:, sl] = (o / l).astype(o_ref.dtype)


def _cross_attention(q, kv, *, batch, seq, mem_len, tq=512):
    T, D = q.shape
    dh = D // CROSS_HEADS
    nq = seq // tq
    return pl.pallas_call(
        functools.partial(_cross_attn_kernel, dh=dh, scale=dh ** -0.5),
        grid=(batch, nq),
        in_specs=[pl.BlockSpec((tq, D), lambda b, i: (b * nq + i, 0)),
                  pl.BlockSpec((mem_len, D), lambda b, i: (b, 0)),
                  pl.BlockSpec((mem_len, D), lambda b, i: (b, 1))],
        out_specs=pl.BlockSpec((tq, D), lambda b, i: (b * nq + i, 0)),
        out_shape=jax.ShapeDtypeStruct((T, D), BF16),
        compiler_params=_params("parallel", "arbitrary"),
        name="cross_attention",
    )(q, kv, kv)


def _merge_exchange_pairs(n):
    t = max(1, (n - 1).bit_length())
    pairs = []
    p = 1 << (t - 1)
    while p > 0:
        q, r, d = 1 << (t - 1), 0, p
        while d > 0:
            pairs += [(i, i + d) for i in range(n - d) if (i & p) == r]
            d, q, r = q - p, q >> 1, p
        p >>= 1
    return pairs


def _top_values(rows, n):
    s = list(rows)
    for i, j in _merge_exchange_pairs(len(s)):
        s[i], s[j] = jnp.maximum(s[i], s[j]), jnp.minimum(s[i], s[j])
    ninf = jnp.full_like(s[0], -jnp.inf)
    vals = []
    for k in range(n):
        m = jnp.max(s[0], axis=0, keepdims=True)
        vals.append(m)
        hit = s[0] == m
        for r in range(min(len(s), n - k - 1)):
            s[r] = jnp.where(hit, s[r + 1] if r + 1 < len(s) else ninf, s[r])
    return vals


def _peer_route_kernel(q_ref, k1_ref, k2_ref, thr_ref, e1_ref, e2_ref):
    K = PEER_TOPK
    q = q_ref[...]
    s1 = lax.dot_general(k1_ref[0], q[:, :PEER_N_KEYS], _NT, preferred_element_type=F32)
    s2 = lax.dot_general(k2_ref[0], q[:, PEER_N_KEYS:], _NT, preferred_element_type=F32)
    v1 = _top_values([s1[r:r + 8] for r in range(0, PEER_N_KEYS, 8)], K + 1)
    v2 = _top_values([s2[r:r + 8] for r in range(0, PEER_N_KEYS, 8)], K + 1)
    v1a, v1b = jnp.concatenate(v1[:8], axis=0), jnp.concatenate(v1[8:16], axis=0)
    v2a, v2b = jnp.concatenate(v2[:8], axis=0), jnp.concatenate(v2[8:16], axis=0)
    ninf = jnp.full_like(v1[0], -jnp.inf)
    cand = [v1[0] + v2a, v1[0] + v2b]
    cand += [v1[a] + v2a for a in range(1, 8)]
    cand += [v1b + v2[0]]
    cand += [jnp.concatenate([v1[K] + v2[0], v1[0] + v2[K]] + [ninf] * 6, axis=0)]
    top = _top_values(cand, K + 1)
    z = jnp.ones_like(top[0])
    for k in range(1, K):
        z = z + jnp.exp(top[k] - top[0])
    tau = 0.5 * (top[K - 1] + top[K])
    thr_ref[0] = jnp.exp((tau - v2[0]) - s1)
    e1_ref[0] = jnp.exp(s1 - v1[0]) * (1.0 / z)
    e2_ref[0] = jnp.exp(s2 - v2[0])


def _peer_route(q, keys1, keys2, *, tm=1024):
    T = q.shape[0]
    H = PEER_HEADS
    kspec = pl.BlockSpec((1, PEER_N_KEYS, PEER_N_KEYS), lambda i, h: (h, 0, 0))
    ospec = pl.BlockSpec((1, PEER_N_KEYS, tm), lambda i, h: (h, 0, i))
    oshape = jax.ShapeDtypeStruct((H, PEER_N_KEYS, T), F32)
    return pl.pallas_call(
        _peer_route_kernel,
        grid=(T // tm, H),
        in_specs=[pl.BlockSpec((tm, 2 * PEER_N_KEYS), lambda i, h: (i, h)), kspec, kspec],
        out_specs=[ospec, ospec, ospec],
        out_shape=[oshape, oshape, oshape],
        compiler_params=_params("parallel", "arbitrary"),
        name="peer_route",
    )(q, keys1, keys2)


def _peer_dense_kernel(xnt_ref, u_ref, v_ref, thr_c, thr_p, e1_c, e1_p, e2_c, e2_p,
                       res_ref, g_ref, o_ref, y_ref, ht0, ht1, wg0, wg1, wg0_prev,
                       *, tm, half, n_exp, n_blocks, n_sub):
    g = pl.program_id(0)
    d_model = y_ref.shape[1]
    kh = half // LANES
    prev = jnp.maximum(g - 1, 0)

    @pl.when(g == 0)
    def _():
        ht1[...] = jnp.zeros(ht1.shape, F32)
        wg0_prev[...] = jnp.zeros(wg0_prev.shape, BF16)

    @pl.when(prev % n_exp == 0)
    def _():
        y_ref[...] = jnp.zeros(y_ref.shape, F32)

    first_valid = (g >= 1).astype(F32)
    second_valid = (g <= n_blocks - 1).astype(F32)

    def pair_step(u_rows, ht_w, ht_r, wg_w, wg_r, thr_ref, e1_ref, e2_ref, key0, v_rows, valid):
        half_valid = 0.5 * valid
        ra, rc = half // n_sub, d_model // n_sub
        for j in range(n_sub):
            for il in range(j * ra // LANES, (j + 1) * ra // LANES):
                rs = slice(il * LANES, (il + 1) * LANES)
                for lt in range(tm // LANES):
                    cs = slice(lt * LANES, (lt + 1) * LANES)
                    gate = jnp.zeros((LANES, LANES), F32)
                    for h in range(PEER_HEADS):
                        e2v = e2_ref[h, :, cs]
                        thr = thr_ref[h, key0 + il:key0 + il + 1, cs]
                        e1 = e1_ref[h, key0 + il:key0 + il + 1, cs]
                        gate = gate + jnp.where(e2v >= thr, e2v, 0.0) * e1
                    hh = ht_r[rs, cs]
                    act = (half_valid * hh) * (1.0 + lax.erf(hh * (2.0 ** -0.5)))
                    wg_w[cs, rs] = (gate * act).T.astype(BF16)
            ht_w[j * ra:(j + 1) * ra, :] = jnp.dot(
                u_ref[u_rows + j * ra:u_rows + (j + 1) * ra, :], xnt_ref[...],
                preferred_element_type=F32)
            y_ref[:, j * rc:(j + 1) * rc] += jnp.dot(
                wg_r[...], v_ref[v_rows:v_rows + half, j * rc:(j + 1) * rc],
                preferred_element_type=F32)

    pair_step(0, ht0, ht1, wg1, wg0_prev, thr_p, e1_p, e2_p, kh, 0, first_valid)
    pair_step(half, ht1, ht0, wg0, wg1, thr_c, e1_c, e2_c, 0, half, second_valid)

    @pl.when(g < n_blocks)
    def _():
        wg0_prev[...] = wg0[...]

    @pl.when(jnp.logical_and(g >= 1, prev % n_exp == n_exp - 1))
    def _():
        h = res_ref[...] + y_ref[...]
        ms = jnp.mean(h * h, axis=-1, keepdims=True)
        o_ref[...] = h * lax.rsqrt(ms + RMS_EPS) * g_ref[...]


def _peer_dense(xnt, u, v, thr, e1, e2, res, g, *, tm=512, te=1024, n_sub=2):
    D, T = xnt.shape
    E = u.shape[0]
    H = PEER_HEADS
    kt = te // LANES
    n_exp = E // te
    n_blocks = (T // tm) * n_exp

    def cur(g):
        b = jnp.minimum(g, n_blocks - 1)
        return b // n_exp, b % n_exp

    def prev(g):
        b = jnp.maximum(g - 1, 0)
        return b // n_exp, b % n_exp

    once = dict(pipeline_mode=pl.Buffered(1))

    def key_spec(f):
        return pl.BlockSpec((H, kt, tm), lambda g: (0, f(g)[1], f(g)[0]))

    def tok_spec(f):
        return pl.BlockSpec((H, PEER_N_KEYS, tm), lambda g: (0, 0, f(g)[0]), **once)

    return pl.pallas_call(
        functools.partial(_peer_dense_kernel, tm=tm, half=te // 2, n_exp=n_exp,
                          n_blocks=n_blocks, n_sub=n_sub),
        grid=(n_blocks + 1,),
        in_specs=[pl.BlockSpec((D, tm), lambda g: (0, cur(g)[0]), **once),
                  pl.BlockSpec((te, D), lambda g: (cur(g)[1], 0)),
                  pl.BlockSpec((te, D), lambda g: (prev(g)[1], 0)),
                  key_spec(cur), key_spec(prev), key_spec(cur), key_spec(prev),
                  tok_spec(cur), tok_spec(prev),
                  pl.BlockSpec((tm, D), lambda g: (prev(g)[0], 0), **once),
                  pl.BlockSpec((1, D), lambda g: (0, 0), **once)],
        out_specs=pl.BlockSpec((tm, D), lambda g: (prev(g)[0], 0)),
        out_shape=jax.ShapeDtypeStruct((T, D), F32),
        scratch_shapes=[pltpu.VMEM((tm, D), F32),
                        pltpu.VMEM((te // 2, tm), F32), pltpu.VMEM((te // 2, tm), F32),
                        pltpu.VMEM((tm, te // 2), BF16), pltpu.VMEM((tm, te // 2), BF16),
                        pltpu.VMEM((tm, te // 2), BF16)],
        compiler_params=_params("arbitrary"),
        name="peer_dense",
    )(xnt, u, v, thr, thr, e1, e1, e2, e2, res, g.reshape(1, D))


def kernel(x, mem, norm_mix_g, w_in, conv_dw_w, conv_dw_b, conv_ln_g, conv_ln_b,
           lambda_q1, lambda_k1, lambda_q2, lambda_k2, diff_subln_g, w_out,
           norm_cross_g, norm_mem_g, w_cq, w_ckv, w_co,
           norm_peer_g, w_pq, peer_keys1, peer_keys2, peer_u, peer_v, final_norm_g):
    B, S, D = x.shape
    T = B * S
    mem_len = mem.shape[1]
    depth = w_in.shape[0]
    conv_w = conv_dw_w.shape[2]
    n_conv = 2 * conv_w
    qk_w = DIFF_HEADS * 2 * DIFF_HEAD_DIM

    assert depth == 1, "the final norm is fused into the PEER kernel of the only layer"
    h = x.reshape(T, D)
    for l in range(depth):
        lam_init = 0.8 - 0.6 * math.exp(-0.3 * l)
        q_scale = DIFF_HEAD_DIM ** -0.5 * math.log2(math.e)
        zc = _norm_matmul(h, norm_mix_g[l], w_in[l], F32, tm=512, tn=n_conv, n_cols=n_conv,
                          name="in_proj_conv")
        zqk = _norm_matmul(h, norm_mix_g[l], w_in[l], BF16, tm=512, tn=2 * qk_w, col0=n_conv,
                           n_cols=2 * qk_w, scaled_cols=qk_w, scale=q_scale, name="in_proj_qk")
        zv = _norm_matmul(h, norm_mix_g[l], w_in[l], BF16, tm=512, tn=conv_w,
                          col0=n_conv + 2 * qk_w, name="in_proj_v")
        conv_out = _conv_group(zc, conv_dw_w[l], conv_dw_b[l], conv_ln_g[l], conv_ln_b[l],
                               batch=B, seq=S)
        attn_out = _diff_attention(zqk, zv, lambda_q1[l], lambda_k1[l], lambda_q2[l], lambda_k2[l],
                                   diff_subln_g[l], batch=B, seq=S, lam_init=lam_init)
        h = _matmul_residual([(conv_out, w_out[l], 0), (attn_out, w_out[l], 1)], h,
                             tm=512, tn=D, name="out_proj")
        kv = _norm_matmul(mem.reshape(B * mem_len, D), norm_mem_g[l], w_ckv[l], BF16,
                          tm=B * mem_len, tn=512, name="cross_kv_proj")
        cq = _norm_matmul(h, norm_cross_g[l], w_cq[l], BF16, tm=512, tn=D, name="cross_q_proj")
        co = _cross_attention(cq, kv, batch=B, seq=S, mem_len=mem_len)
        h = _matmul_residual([(co, w_co[l], 0)], h, tm=512, tn=D, name="cross_out_proj")
        pq, xnt = _norm_matmul(h, norm_peer_g[l], w_pq[l], BF16, tm=512, tn=w_pq.shape[2],
                               emit_transposed=True, name="peer_q_proj")
        thr, e1, e2 = _peer_route(pq, peer_keys1[l].astype(BF16), peer_keys2[l].astype(BF16))
        h = _peer_dense(xnt, peer_u[l].astype(BF16), peer_v[l].astype(BF16),
                        thr, e1, e2, h, final_norm_g)
    return h.reshape(B, S, D)
```

```python
import functools
import math

import jax
import jax.numpy as jnp
from jax import lax
from jax.experimental import pallas as pl
from jax.experimental.pallas import tpu as pltpu

F32 = jnp.float32
BF16 = jnp.bfloat16

RMS_EPS = 1e-6
LN_EPS = 1e-5
NEG_INF = -1e30

LANES = 128
CONV_KERNEL = 31
CONV_HALO = 32
DIFF_HEADS = 8
DIFF_HEAD_DIM = 64
CROSS_HEADS = 4
PEER_HEADS = 8
PEER_N_KEYS = 128
PEER_TOPK = 16
VMEM_LIMIT = 56 * 1024 * 1024

_NT = (((1,), (1,)), ((), ()))


def _params(*sem):
    return pltpu.CompilerParams(dimension_semantics=sem, vmem_limit_bytes=VMEM_LIMIT)


def _bf16_weights(w_ref, cache_ref):
    if cache_ref is None:
        return w_ref[...].astype(BF16)

    @pl.when(jnp.logical_and(pl.program_id(0) == 0, pl.program_id(1) == 0))
    def _():
        cache_ref[...] = w_ref[...].astype(BF16)

    return cache_ref[...]


def _norm_mm_kernel(x_ref, g_ref, w_ref, *rest, emit_transposed, scaled_cols, scale):
    o_ref, hn_ref = rest[0], rest[-1]

    @pl.when(pl.program_id(1) == 0)
    def _():
        x = x_ref[...]
        ms = jnp.mean(x * x, axis=-1, keepdims=True)
        hn = x * lax.rsqrt(ms + RMS_EPS) * g_ref[...]
        hn_ref[...] = hn.astype(BF16)
        if emit_transposed:
            rest[1][...] = hn.T.astype(BF16)

    acc = jnp.dot(hn_ref[...], w_ref[...].astype(BF16), preferred_element_type=F32)
    if scaled_cols:
        first = jnp.where(pl.program_id(1) == 0, scale, 1.0)
        o_ref[:, :scaled_cols] = (acc[:, :scaled_cols] * first).astype(o_ref.dtype)
        if scaled_cols < acc.shape[1]:
            o_ref[:, scaled_cols:] = acc[:, scaled_cols:].astype(o_ref.dtype)
    else:
        o_ref[...] = acc.astype(o_ref.dtype)


def _norm_matmul(x, g, w, out_dtype, *, tm, tn, name, col0=0, n_cols=None,
                 emit_transposed=False, scaled_cols=0, scale=1.0):
    M, K = x.shape
    n_cols = w.shape[1] - col0 if n_cols is None else n_cols
    assert col0 % tn == 0 and scaled_cols <= tn
    cb0 = col0 // tn
    w_mode = dict(pipeline_mode=pl.Buffered(1)) if n_cols == tn else {}
    out_specs = [pl.BlockSpec((tm, tn), lambda i, j: (i, j))]
    out_shape = [jax.ShapeDtypeStruct((M, n_cols), out_dtype)]
    if emit_transposed:
        out_specs.append(pl.BlockSpec((K, tm), lambda i, j: (0, i)))
        out_shape.append(jax.ShapeDtypeStruct((K, M), BF16))
    out = pl.pallas_call(
        functools.partial(_norm_mm_kernel, emit_transposed=emit_transposed,
                          scaled_cols=scaled_cols, scale=scale),
        grid=(M // tm, n_cols // tn),
        in_specs=[pl.BlockSpec((tm, K), lambda i, j: (i, 0)),
                  pl.BlockSpec((1, K), lambda i, j: (0, 0)),
                  pl.BlockSpec((K, tn), lambda i, j: (0, j + cb0), **w_mode)],
        out_specs=out_specs,
        out_shape=out_shape,
        scratch_shapes=[pltpu.VMEM((tm, K), BF16)],
        compiler_params=_params("parallel", "arbitrary"),
        name=name,
    )(x, g.reshape(1, K), w)
    return out if emit_transposed else out[0]


def _mm_res_kernel(*refs, n_pairs, resident):
    res_ref, o_ref = refs[2 * n_pairs], refs[2 * n_pairs + 1]
    caches = refs[2 * n_pairs + 2:] if resident else (None,) * n_pairs
    acc = res_ref[...]
    for p in range(n_pairs):
        acc = acc + jnp.dot(refs[2 * p][...], _bf16_weights(refs[2 * p + 1], caches[p]),
                            preferred_element_type=F32)
    o_ref[...] = acc


def _matmul_residual(pairs, res, *, tm, tn, name):
    M, N = res.shape
    resident = tn == N
    w_mode = dict(pipeline_mode=pl.Buffered(1)) if resident else {}
    in_specs, args, scratch = [], [], []
    for a, w, rb in pairs:
        k = a.shape[1]
        in_specs += [pl.BlockSpec((tm, k), lambda i, j: (i, 0)),
                     pl.BlockSpec((k, tn), lambda i, j, rb=rb: (rb, j), **w_mode)]
        args += [a, w]
        if resident:
            scratch.append(pltpu.VMEM((k, tn), BF16))
    in_specs.append(pl.BlockSpec((tm, tn), lambda i, j: (i, j)))
    return pl.pallas_call(
        functools.partial(_mm_res_kernel, n_pairs=len(pairs), resident=resident),
        grid=(M // tm, N // tn),
        in_specs=in_specs,
        out_specs=pl.BlockSpec((tm, tn), lambda i, j: (i, j)),
        out_shape=jax.ShapeDtypeStruct((M, N), F32),
        scratch_shapes=scratch,
        compiler_params=_params("arbitrary" if resident else "parallel", "arbitrary"),
        name=name,
    )(*args, res)


def _conv_kernel(a_ref, gt_ref, ah_ref, gh_ref, w_ref, b_ref, lg_ref, lb_ref, o_ref, ubuf,
                 *, ts, rows):
    i = pl.program_id(2)
    uh = ah_ref[...] * jax.nn.sigmoid(gh_ref[...])
    ubuf[0:CONV_HALO, :] = jnp.where(i > 0, uh, 0.0)
    ubuf[CONV_HALO:, :] = a_ref[...] * jax.nn.sigmoid(gt_ref[...])
    bias = b_ref[...]
    lg = lg_ref[...]
    lb = lb_ref[...]
    first = CONV_HALO - (CONV_KERNEL - 1)
    for r in range(ts // rows):
        acc = jnp.zeros((rows, LANES), F32) + bias
        for j in range(CONV_KERNEL):
            start = r * rows + first + j
            acc = acc + w_ref[j:j + 1, :] * ubuf[start:start + rows, :]
        mu = jnp.mean(acc, axis=-1, keepdims=True)
        d = acc - mu
        var = jnp.mean(d * d, axis=-1, keepdims=True)
        y = d * lax.rsqrt(var + LN_EPS) * lg + lb
        o_ref[r * rows:(r + 1) * rows, :] = (y * jax.nn.sigmoid(y)).astype(o_ref.dtype)


def _conv_group(zc, w_dw, b_dw, ln_g, ln_b, *, batch, seq, ts=512, rows=128):
    T, C2 = zc.shape
    C = C2 // 2
    G = C // LANES
    ns = seq // ts
    hb = ts // CONV_HALO

    def main(off):
        return pl.BlockSpec((ts, LANES), lambda b, g, i: (b * ns + i, g + off))

    def halo(off):
        return pl.BlockSpec(
            (CONV_HALO, LANES),
            lambda b, g, i: (jnp.maximum((b * ns + i) * hb - 1, 0), g + off))

    vec = pl.BlockSpec((1, LANES), lambda b, g, i: (0, g))
    return pl.pallas_call(
        functools.partial(_conv_kernel, ts=ts, rows=rows),
        grid=(batch, G, ns),
        in_specs=[main(0), main(G), halo(0), halo(G),
                  pl.BlockSpec((CONV_KERNEL, LANES), lambda b, g, i: (0, g)),
                  vec, vec, vec],
        out_specs=pl.BlockSpec((ts, LANES), lambda b, g, i: (b * ns + i, g)),
        out_shape=jax.ShapeDtypeStruct((T, C), BF16),
        scratch_shapes=[pltpu.VMEM((ts + CONV_HALO, LANES), F32)],
        compiler_params=_params("parallel", "parallel", "arbitrary"),
        name="conv_group",
    )(zc, zc, zc, zc, w_dw, b_dw.reshape(1, C), ln_g.reshape(1, C), ln_b.reshape(1, C))


def _diff_attn_kernel(q_ref, k_ref, v_ref, lq1, lk1, lq2, lk2, sg_ref, o_ref,
                      qs_ref, vx_ref, m_ref, acc_ref, *, tq, rows, lam_init):
    qi = pl.program_id(2)
    per = tq // rows
    n_groups = 2 * per

    @pl.when(qi == 0)
    def _():
        vx_ref[:, :LANES] = v_ref[...]
        vx_ref[:, LANES:] = jnp.ones((vx_ref.shape[0], LANES), BF16)

    q = q_ref[...]
    lane = lax.broadcasted_iota(jnp.int32, (tq, LANES), 1)
    zero = jnp.zeros_like(q)
    for c, qc in enumerate((jnp.where(lane < DIFF_HEAD_DIM, q, zero),
                            jnp.where(lane >= DIFF_HEAD_DIM, q, zero))):
        for r in range(per):
            qs_ref[c * per + r] = qc[r * rows:(r + 1) * rows]
    m_ref[...] = jnp.full(m_ref.shape, NEG_INF, F32)
    acc_ref[...] = jnp.zeros(acc_ref.shape, F32)

    def step(ki, masked):
        start = pl.multiple_of(ki * tq, tq)
        kb = k_ref[pl.ds(start, tq), :]
        vb = vx_ref[pl.ds(start, tq), :]
        for g in range(n_groups):
            s = lax.dot_general(qs_ref[g], kb, _NT, preferred_element_type=F32)
            if masked:
                row = lax.broadcasted_iota(jnp.int32, (rows, tq), 0) + (g % per) * rows
                col = lax.broadcasted_iota(jnp.int32, (rows, tq), 1)
                s = jnp.where(col <= row, s, NEG_INF)
            m_old = m_ref[g]
            m_new = jnp.maximum(m_old, jnp.max(s, axis=-1, keepdims=True))
            alpha = jnp.exp2(m_old - m_new)
            p = jnp.exp2(s - jnp.tile(m_new, (1, tq // LANES)))
            acc_ref[g] = (jnp.tile(alpha, (1, 2)) * acc_ref[g]
                          + jnp.dot(p.astype(BF16), vb, preferred_element_type=F32))
            m_ref[g] = m_new

    def body(k2, c):
        step(2 * k2, False)
        step(2 * k2 + 1, False)
        return c

    lax.fori_loop(0, qi // 2, body, 0)

    @pl.when(qi % 2 == 1)
    def _():
        step(qi - 1, False)

    step(qi, True)

    lam = (jnp.exp(jnp.sum(lq1[...] * lk1[...], axis=-1, keepdims=True))
           - jnp.exp(jnp.sum(lq2[...] * lk2[...], axis=-1, keepdims=True)) + lam_init)
    for r in range(per):
        a1, a2 = acc_ref[r], acc_ref[per + r]
        of = a1[:, :LANES] / a1[:, LANES:] - lam * (a2[:, :LANES] / a2[:, LANES:])
        of = of * lax.rsqrt(jnp.mean(of * of, axis=-1, keepdims=True) + RMS_EPS)
        of = of * sg_ref[...] * (1.0 - lam_init)
        o_ref[r * rows:(r + 1) * rows, :] = of.astype(o_ref.dtype)


def _diff_attention(zqk, zv, lq1, lk1, lq2, lk2, subln_g, *, batch, seq, lam_init,
                    tq=512, rows=128):
    T = zqk.shape[0]
    H = DIFF_HEADS
    nq = seq // tq
    n_groups = 2 * tq // rows
    lam_spec = pl.BlockSpec((1, DIFF_HEAD_DIM), lambda b, h, i: (0, 0))
    return pl.pallas_call(
        functools.partial(_diff_attn_kernel, tq=tq, rows=rows, lam_init=lam_init),
        grid=(batch, H, nq),
        in_specs=[pl.BlockSpec((tq, LANES), lambda b, h, i: (b * nq + i, h)),
                  pl.BlockSpec((seq, LANES), lambda b, h, i: (b, H + h)),
                  pl.BlockSpec((seq, LANES), lambda b, h, i: (b, h)),
                  lam_spec, lam_spec, lam_spec, lam_spec,
                  pl.BlockSpec((1, LANES), lambda b, h, i: (0, 0))],
        out_specs=pl.BlockSpec((tq, LANES), lambda b, h, i: (b * nq + i, h)),
        out_shape=jax.ShapeDtypeStruct((T, H * LANES), BF16),
        scratch_shapes=[pltpu.VMEM((n_groups, rows, LANES), BF16),
                        pltpu.VMEM((seq, 2 * LANES), BF16),
                        pltpu.VMEM((n_groups, rows, LANES), F32),
                        pltpu.VMEM((n_groups, rows, 2 * LANES), F32)],
        compiler_params=_params("parallel", "parallel", "arbitrary"),
        name="diff_attention",
    )(zqk, zqk, zv,
      lq1.reshape(1, -1), lk1.reshape(1, -1), lq2.reshape(1, -1), lk2.reshape(1, -1),
      subln_g.reshape(1, -1))


def _cross_attn_kernel(q_ref, k_ref, v_ref, o_ref, *, dh, scale):
    for h in range(CROSS_HEADS):
        sl = slice(h * dh, (h + 1) * dh)
        s = lax.dot_general(q_ref[:, sl], k_ref[:, sl], _NT,
                            preferred_element_type=F32) * scale
        p = jnp.exp(s - jnp.max(s, axis=-1, keepdims=True))
        l = jnp.sum(p, axis=-1, keepdims=True)
        o = jnp.dot(p.astype(BF16), v_ref[:, sl], preferred_element_type=F32)
        o_ref[:, sl] = (o / l).astype(o_ref.dtype)


def _cross_attention(q, kv, *, batch, seq, mem_len, tq=512):
    T, D = q.shape
    dh = D // CROSS_HEADS
    nq = seq // tq
    return pl.pallas_call(
        functools.partial(_cross_attn_kernel, dh=dh, scale=dh ** -0.5),
        grid=(batch, nq),
        in_specs=[pl.BlockSpec((tq, D), lambda b, i: (b * nq + i, 0)),
                  pl.BlockSpec((mem_len, D), lambda b, i: (b, 0)),
                  pl.BlockSpec((mem_len, D), lambda b, i: (b, 1))],
        out_specs=pl.BlockSpec((tq, D), lambda b, i: (b * nq + i, 0)),
        out_shape=jax.ShapeDtypeStruct((T, D), BF16),
        compiler_params=_params("parallel", "arbitrary"),
        name="cross_attention",
    )(q, kv, kv)


def _merge_exchange_pairs(n):
    t = max(1, (n - 1).bit_length())
    pairs = []
    p = 1 << (t - 1)
    while p > 0:
        q, r, d = 1 << (t - 1), 0, p
        while d > 0:
            pairs += [(i, i + d) for i in range(n - d) if (i & p) == r]
            d, q, r = q - p, q >> 1, p
        p >>= 1
    return pairs


def _top_values(rows, n):
    s = list(rows)
    for i, j in _merge_exchange_pairs(len(s)):
        s[i], s[j] = jnp.maximum(s[i], s[j]), jnp.minimum(s[i], s[j])
    ninf = jnp.full_like(s[0], -jnp.inf)
    vals = []
    for k in range(n):
        m = jnp.max(s[0], axis=0, keepdims=True)
        vals.append(m)
        hit = s[0] == m
        for r in range(min(len(s), n - k - 1)):
            s[r] = jnp.where(hit, s[r + 1] if r + 1 < len(s) else ninf, s[r])
    return vals


def _peer_route_kernel(q_ref, k1_ref, k2_ref, thr_ref, e1_ref, e2_ref):
    K = PEER_TOPK
    q = q_ref[...]
    s1 = lax.dot_general(k1_ref[0], q[:, :PEER_N_KEYS], _NT, preferred_element_type=F32)
    s2 = lax.dot_general(k2_ref[0], q[:, PEER_N_KEYS:], _NT, preferred_element_type=F32)
    v1 = _top_values([s1[r:r + 8] for r in range(0, PEER_N_KEYS, 8)], K + 1)
    v2 = _top_values([s2[r:r + 8] for r in range(0, PEER_N_KEYS, 8)], K + 1)
    v1a, v1b = jnp.concatenate(v1[:8], axis=0), jnp.concatenate(v1[8:16], axis=0)
    v2a, v2b = jnp.concatenate(v2[:8], axis=0), jnp.concatenate(v2[8:16], axis=0)
    ninf = jnp.full_like(v1[0], -jnp.inf)
    cand = [v1[0] + v2a, v1[0] + v2b]
    cand += [v1[a] + v2a for a in range(1, 8)]
    cand += [v1b + v2[0]]
    cand += [jnp.concatenate([v1[K] + v2[0], v1[0] + v2[K]] + [ninf] * 6, axis=0)]
    top = _top_values(cand, K + 1)
    z = jnp.ones_like(top[0])
    for k in range(1, K):
        z = z + jnp.exp(top[k] - top[0])
    tau = 0.5 * (top[K - 1] + top[K])
    thr_ref[0] = jnp.exp((tau - v2[0]) - s1)
    e1_ref[0] = jnp.exp(s1 - v1[0]) * (1.0 / z)
    e2_ref[0] = jnp.exp(s2 - v2[0])


def _peer_route(q, keys1, keys2, *, tm=1024):
    T = q.shape[0]
    H = PEER_HEADS
    kspec = pl.BlockSpec((1, PEER_N_KEYS, PEER_N_KEYS), lambda i, h: (h, 0, 0))
    ospec = pl.BlockSpec((1, PEER_N_KEYS, tm), lambda i, h: (h, 0, i))
    oshape = jax.ShapeDtypeStruct((H, PEER_N_KEYS, T), F32)
    return pl.pallas_call(
        _peer_route_kernel,
        grid=(T // tm, H),
        in_specs=[pl.BlockSpec((tm, 2 * PEER_N_KEYS), lambda i, h: (i, h)), kspec, kspec],
        out_specs=[ospec, ospec, ospec],
        out_shape=[oshape, oshape, oshape],
        compiler_params=_params("parallel", "arbitrary"),
        name="peer_route",
    )(q, keys1, keys2)


def _peer_dense_kernel(xnt_ref, u_ref, v_ref, thr_c, thr_p, e1_c, e1_p, e2_c, e2_p,
                       res_ref, g_ref, o_ref, y_ref, ht0, ht1, wg0, wg1, wg0_prev,
                       *, tm, half, n_exp, n_blocks, n_sub):
    g = pl.program_id(0)
    d_model = y_ref.shape[1]
    kh = half // LANES
    prev = jnp.maximum(g - 1, 0)

    @pl.when(g == 0)
    def _():
        ht1[...] = jnp.zeros(ht1.shape, F32)
        wg0_prev[...] = jnp.zeros(wg0_prev.shape, BF16)

    @pl.when(prev % n_exp == 0)
    def _():
        y_ref[...] = jnp.zeros(y_ref.shape, F32)

    first_valid = (g >= 1).astype(F32)
    second_valid = (g <= n_blocks - 1).astype(F32)

    def pair_step(u_rows, ht_w, ht_r, wg_w, wg_r, thr_ref, e1_ref, e2_ref, key0, v_rows, valid):
        half_valid = 0.5 * valid
        ra, rc = half // n_sub, d_model // n_sub
        for j in range(n_sub):
            for il in range(j * ra // LANES, (j + 1) * ra // LANES):
                rs = slice(il * LANES, (il + 1) * LANES)
                for lt in range(tm // LANES):
                    cs = slice(lt * LANES, (lt + 1) * LANES)
                    gate = jnp.zeros((LANES, LANES), F32)
                    for h in range(PEER_HEADS):
                        e2v = e2_ref[h, :, cs]
                        thr = thr_ref[h, key0 + il:key0 + il + 1, cs]
                        e1 = e1_ref[h, key0 + il:key0 + il + 1, cs]
                        gate = gate + jnp.where(e2v >= thr, e2v, 0.0) * e1
                    hh = ht_r[rs, cs]
                    act = (half_valid * hh) * (1.0 + lax.erf(hh * (2.0 ** -0.5)))
                    wg_w[cs, rs] = (gate * act).T.astype(BF16)
            ht_w[j * ra:(j + 1) * ra, :] = jnp.dot(
                u_ref[u_rows + j * ra:u_rows + (j + 1) * ra, :], xnt_ref[...],
                preferred_element_type=F32)
            y_ref[:, j * rc:(j + 1) * rc] += jnp.dot(
                wg_r[...], v_ref[v_rows:v_rows + half, j * rc:(j + 1) * rc],
                preferred_element_type=F32)

    pair_step(0, ht0, ht1, wg1, wg0_prev, thr_p, e1_p, e2_p, kh, 0, first_valid)
    pair_step(half, ht1, ht0, wg0, wg1, thr_c, e1_c, e2_c, 0, half, second_valid)

    @pl.when(g < n_blocks)
    def _():
        wg0_prev[...] = wg0[...]

    @pl.when(jnp.logical_and(g >= 1, prev % n_exp == n_exp - 1))
    def _():
        h = res_ref[...] + y_ref[...]
        ms = jnp.mean(h * h, axis=-1, keepdims=True)
        o_ref[...] = h * lax.rsqrt(ms + RMS_EPS) * g_ref[...]


def _peer_dense(xnt, u, v, thr, e1, e2, res, g, *, tm=512, te=1024, n_sub=2):
    D, T = xnt.shape
    E = u.shape[0]
    H = PEER_HEADS
    kt = te // LANES
    n_exp = E // te
    n_blocks = (T // tm) * n_exp

    def cur(g):
        b = jnp.minimum(g, n_blocks - 1)
        return b // n_exp, b % n_exp

    def prev(g):
        b = jnp.maximum(g - 1, 0)
        return b // n_exp, b % n_exp

    once = dict(pipeline_mode=pl.Buffered(1))

    def key_spec(f):
        return pl.BlockSpec((H, kt, tm), lambda g: (0, f(g)[1], f(g)[0]))

    def tok_spec(f):
        return pl.BlockSpec((H, PEER_N_KEYS, tm), lambda g: (0, 0, f(g)[0]), **once)

    return pl.pallas_call(
        functools.partial(_peer_dense_kernel, tm=tm, half=te // 2, n_exp=n_exp,
                          n_blocks=n_blocks, n_sub=n_sub),
        grid=(n_blocks + 1,),
        in_specs=[pl.BlockSpec((D, tm), lambda g: (0, cur(g)[0]), **once),
                  pl.BlockSpec((te, D), lambda g: (cur(g)[1], 0)),
                  pl.BlockSpec((te, D), lambda g: (prev(g)[1], 0)),
                  key_spec(cur), key_spec(prev), key_spec(cur), key_spec(prev),
                  tok_spec(cur), tok_spec(prev),
                  pl.BlockSpec((tm, D), lambda g: (prev(g)[0], 0), **once),
                  pl.BlockSpec((1, D), lambda g: (0, 0), **once)],
        out_specs=pl.BlockSpec((tm, D), lambda g: (prev(g)[0], 0)),
        out_shape=jax.ShapeDtypeStruct((T, D), F32),
        scratch_shapes=[pltpu.VMEM((tm, D), F32),
                        pltpu.VMEM((te // 2, tm), F32), pltpu.VMEM((te // 2, tm), F32),
                        pltpu.VMEM((tm, te // 2), BF16), pltpu.VMEM((tm, te // 2), BF16),
                        pltpu.VMEM((tm, te // 2), BF16)],
        compiler_params=_params("arbitrary"),
        name="peer_dense",
    )(xnt, u, v, thr, thr, e1, e1, e2, e2, res, g.reshape(1, D))


def kernel(x, mem, norm_mix_g, w_in, conv_dw_w, conv_dw_b, conv_ln_g, conv_ln_b,
           lambda_q1, lambda_k1, lambda_q2, lambda_k2, diff_subln_g, w_out,
           norm_cross_g, norm_mem_g, w_cq, w_ckv, w_co,
           norm_peer_g, w_pq, peer_keys1, peer_keys2, peer_u, peer_v, final_norm_g):
    B, S, D = x.shape
    T = B * S
    mem_len = mem.shape[1]
    depth = w_in.shape[0]
    conv_w = conv_dw_w.shape[2]
    n_conv = 2 * conv_w
    qk_w = DIFF_HEADS * 2 * DIFF_HEAD_DIM

    assert depth == 1, "the final norm is fused into the PEER kernel of the only layer"
    h = x.reshape(T, D)
    for l in range(depth):
        lam_init = 0.8 - 0.6 * math.exp(-0.3 * l)
        q_scale = DIFF_HEAD_DIM ** -0.5 * math.log2(math.e)
        zc = _norm_matmul(h, norm_mix_g[l], w_in[l], F32, tm=512, tn=n_conv, n_cols=n_conv,
                          name="in_proj_conv")
        zqk = _norm_matmul(h, norm_mix_g[l], w_in[l], BF16, tm=512, tn=2 * qk_w, col0=n_conv,
                           n_cols=2 * qk_w, scaled_cols=qk_w, scale=q_scale, name="in_proj_qk")
        zv = _norm_matmul(h, norm_mix_g[l], w_in[l], BF16, tm=512, tn=conv_w,
                          col0=n_conv + 2 * qk_w, name="in_proj_v")
        conv_out = _conv_group(zc, conv_dw_w[l], conv_dw_b[l], conv_ln_g[l], conv_ln_b[l],
                               batch=B, seq=S)
        attn_out = _diff_attention(zqk, zv, lambda_q1[l], lambda_k1[l], lambda_q2[l], lambda_k2[l],
                                   diff_subln_g[l], batch=B, seq=S, lam_init=lam_init)
        h = _matmul_residual([(conv_out, w_out[l], 0), (attn_out, w_out[l], 1)], h,
                             tm=512, tn=D, name="out_proj")
        kv = _norm_matmul(mem.reshape(B * mem_len, D), norm_mem_g[l], w_ckv[l], BF16,
                          tm=B * mem_len, tn=512, name="cross_kv_proj")
        cq = _norm_matmul(h, norm_cross_g[l], w_cq[l], BF16, tm=512, tn=D, name="cross_q_proj")
        co = _cross_attention(cq, kv, batch=B, seq=S, mem_len=mem_len)
        h = _matmul_residual([(co, w_co[l], 0)], h, tm=512, tn=D, name="cross_out_proj")
        pq, xnt = _norm_matmul(h, norm_peer_g[l], w_pq[l], BF16, tm=512, tn=w_pq.shape[2],
                               emit_transposed=True, name="peer_q_proj")
        thr, e1, e2 = _peer_route(pq, peer_keys1[l].astype(BF16), peer_keys2[l].astype(BF16))
        h = _peer_dense(xnt, peer_u[l].astype(BF16), peer_v[l].astype(BF16),
                        thr, e1, e2, h, final_norm_g)
    return h.reshape(B, S, D)
```

```python
import functools
import math

import jax
import jax.numpy as jnp
from jax import lax
from jax.experimental import pallas as pl
from jax.experimental.pallas import tpu as pltpu

F32 = jnp.float32
BF16 = jnp.bfloat16

RMS_EPS = 1e-6
LN_EPS = 1e-5
NEG_INF = -1e30

LANES = 128
SUBLANES = 8
CONV_KERNEL = 31
CONV_HALO = 32
DIFF_HEADS = 8
DIFF_HEAD_DIM = 64
CROSS_HEADS = 4
PEER_HEADS = 8
PEER_N_KEYS = 128
PEER_TOPK = 16
VMEM_LIMIT = 56 * 1024 * 1024

_NT = (((1,), (1,)), ((), ()))


def _params(*sem):
    return pltpu.CompilerParams(dimension_semantics=sem, vmem_limit_bytes=VMEM_LIMIT)


def _norm_mm_kernel(x_ref, g_ref, w_ref, *rest, emit_transposed, scaled_cols, scale):
    o_ref, hn_ref = rest[0], rest[-1]

    @pl.when(pl.program_id(1) == 0)
    def _():
        x = x_ref[...]
        ms = jnp.mean(x * x, axis=-1, keepdims=True)
        hn = x * lax.rsqrt(ms + RMS_EPS) * g_ref[...]
        hn_ref[...] = hn.astype(BF16)
        if emit_transposed:
            rest[1][...] = hn.T.astype(BF16)

    acc = jnp.dot(hn_ref[...], w_ref[...].astype(BF16), preferred_element_type=F32)
    if scaled_cols:
        first = jnp.where(pl.program_id(1) == 0, scale, 1.0)
        o_ref[:, :scaled_cols] = (acc[:, :scaled_cols] * first).astype(o_ref.dtype)
        if scaled_cols < acc.shape[1]:
            o_ref[:, scaled_cols:] = acc[:, scaled_cols:].astype(o_ref.dtype)
    else:
        o_ref[...] = acc.astype(o_ref.dtype)


def _norm_matmul(x, g, w, out_dtype, *, tm, tn, name, col0=0, n_cols=None,
                 emit_transposed=False, scaled_cols=0, scale=1.0):
    M, K = x.shape
    n_cols = w.shape[1] - col0 if n_cols is None else n_cols
    assert col0 % tn == 0 and scaled_cols <= tn
    cb0 = col0 // tn
    w_mode = dict(pipeline_mode=pl.Buffered(1)) if n_cols == tn else {}
    out_specs = [pl.BlockSpec((tm, tn), lambda i, j: (i, j))]
    out_shape = [jax.ShapeDtypeStruct((M, n_cols), out_dtype)]
    if emit_transposed:
        out_specs.append(pl.BlockSpec((K, tm), lambda i, j: (0, i)))
        out_shape.append(jax.ShapeDtypeStruct((K, M), BF16))
    out = pl.pallas_call(
        functools.partial(_norm_mm_kernel, emit_transposed=emit_transposed,
                          scaled_cols=scaled_cols, scale=scale),
        grid=(M // tm, n_cols // tn),
        in_specs=[pl.BlockSpec((tm, K), lambda i, j: (i, 0)),
                  pl.BlockSpec((1, K), lambda i, j: (0, 0)),
                  pl.BlockSpec((K, tn), lambda i, j: (0, j + cb0), **w_mode)],
        out_specs=out_specs,
        out_shape=out_shape,
        scratch_shapes=[pltpu.VMEM((tm, K), BF16)],
        compiler_params=_params("parallel", "arbitrary"),
        name=name,
    )(x, g.reshape(1, K), w)
    return out if emit_transposed else out[0]


def _mm_res_kernel(*refs, n_pairs):
    res_ref, o_ref = refs[2 * n_pairs], refs[2 * n_pairs + 1]
    acc = res_ref[...]
    for p in range(n_pairs):
        acc = acc + jnp.dot(refs[2 * p][...], refs[2 * p + 1][...].astype(BF16),
                            preferred_element_type=F32)
    o_ref[...] = acc


def _matmul_residual(pairs, res, *, tm, tn, name):
    M, N = res.shape
    w_mode = dict(pipeline_mode=pl.Buffered(1)) if tn == N else {}
    in_specs, args = [], []
    for a, w, rb in pairs:
        k = a.shape[1]
        in_specs += [pl.BlockSpec((tm, k), lambda i, j: (i, 0)),
                     pl.BlockSpec((k, tn), lambda i, j, rb=rb: (rb, j), **w_mode)]
        args += [a, w]
    in_specs.append(pl.BlockSpec((tm, tn), lambda i, j: (i, j)))
    return pl.pallas_call(
        functools.partial(_mm_res_kernel, n_pairs=len(pairs)),
        grid=(M // tm, N // tn),
        in_specs=in_specs,
        out_specs=pl.BlockSpec((tm, tn), lambda i, j: (i, j)),
        out_shape=jax.ShapeDtypeStruct((M, N), F32),
        compiler_params=_params("parallel", "arbitrary"),
        name=name,
    )(*args, res)


def _conv_kernel(a_ref, gt_ref, ah_ref, gh_ref, w_ref, b_ref, lg_ref, lb_ref, o_ref, ubuf,
                 *, ts, rows):
    i = pl.program_id(2)
    uh = ah_ref[...] * jax.nn.sigmoid(gh_ref[...])
    ubuf[0:CONV_HALO, :] = jnp.where(i > 0, uh, 0.0)
    ubuf[CONV_HALO:, :] = a_ref[...] * jax.nn.sigmoid(gt_ref[...])
    bias = b_ref[...]
    lg = lg_ref[...]
    lb = lb_ref[...]
    first = CONV_HALO - (CONV_KERNEL - 1)
    for r in range(ts // rows):
        acc = jnp.zeros((rows, LANES), F32) + bias
        for j in range(CONV_KERNEL):
            start = r * rows + first + j
            acc = acc + w_ref[j:j + 1, :] * ubuf[start:start + rows, :]
        mu = jnp.mean(acc, axis=-1, keepdims=True)
        d = acc - mu
        var = jnp.mean(d * d, axis=-1, keepdims=True)
        y = d * lax.rsqrt(var + LN_EPS) * lg + lb
        o_ref[r * rows:(r + 1) * rows, :] = (y * jax.nn.sigmoid(y)).astype(o_ref.dtype)


def _conv_group(zc, w_dw, b_dw, ln_g, ln_b, *, batch, seq, ts=512, rows=128):
    T, C2 = zc.shape
    C = C2 // 2
    G = C // LANES
    ns = seq // ts
    hb = ts // CONV_HALO

    def main(off):
        return pl.BlockSpec((ts, LANES), lambda b, g, i: (b * ns + i, g + off))

    def halo(off):
        return pl.BlockSpec(
            (CONV_HALO, LANES),
            lambda b, g, i: (jnp.maximum((b * ns + i) * hb - 1, 0), g + off))

    vec = pl.BlockSpec((1, LANES), lambda b, g, i: (0, g))
    return pl.pallas_call(
        functools.partial(_conv_kernel, ts=ts, rows=rows),
        grid=(batch, G, ns),
        in_specs=[main(0), main(G), halo(0), halo(G),
                  pl.BlockSpec((CONV_KERNEL, LANES), lambda b, g, i: (0, g)),
                  vec, vec, vec],
        out_specs=pl.BlockSpec((ts, LANES), lambda b, g, i: (b * ns + i, g)),
        out_shape=jax.ShapeDtypeStruct((T, C), BF16),
        scratch_shapes=[pltpu.VMEM((ts + CONV_HALO, LANES), F32)],
        compiler_params=_params("parallel", "parallel", "arbitrary"),
        name="conv_group",
    )(zc, zc, zc, zc, w_dw, b_dw.reshape(1, C), ln_g.reshape(1, C), ln_b.reshape(1, C))


def _diff_attn_kernel(q_ref, k_ref, v_ref, lq1, lk1, lq2, lk2, sg_ref, o_ref,
                      qs_ref, vx_ref, m_ref, acc_ref, *, tq, rows, lam_init):
    qi = pl.program_id(2)
    per = tq // rows
    n_groups = 2 * per

    @pl.when(qi == 0)
    def _():
        vx_ref[:, :LANES] = v_ref[...]
        vx_ref[:, LANES:] = jnp.ones((vx_ref.shape[0], LANES), BF16)

    q = q_ref[...]
    lane = lax.broadcasted_iota(jnp.int32, (tq, LANES), 1)
    zero = jnp.zeros_like(q)
    for c, qc in enumerate((jnp.where(lane < DIFF_HEAD_DIM, q, zero),
                            jnp.where(lane >= DIFF_HEAD_DIM, q, zero))):
        for r in range(per):
            qs_ref[c * per + r] = qc[r * rows:(r + 1) * rows]
    m_ref[...] = jnp.full(m_ref.shape, NEG_INF, F32)
    acc_ref[...] = jnp.zeros(acc_ref.shape, F32)

    def step(ki, masked):
        start = pl.multiple_of(ki * tq, tq)
        kb = k_ref[pl.ds(start, tq), :]
        vb = vx_ref[pl.ds(start, tq), :]
        for g in range(n_groups):
            s = lax.dot_general(qs_ref[g], kb, _NT, preferred_element_type=F32)
            if masked:
                row = lax.broadcasted_iota(jnp.int32, (rows, tq), 0) + (g % per) * rows
                col = lax.broadcasted_iota(jnp.int32, (rows, tq), 1)
                s = jnp.where(col <= row, s, NEG_INF)
            m_old = m_ref[g]
            m_new = jnp.maximum(m_old, jnp.max(s, axis=-1, keepdims=True))
            alpha = jnp.exp2(m_old - m_new)
            p = jnp.exp2(s - jnp.tile(m_new, (1, tq // LANES)))
            acc_ref[g] = (jnp.tile(alpha, (1, 2)) * acc_ref[g]
                          + jnp.dot(p.astype(BF16), vb, preferred_element_type=F32))
            m_ref[g] = m_new

    def body(k2, c):
        step(2 * k2, False)
        step(2 * k2 + 1, False)
        return c

    lax.fori_loop(0, qi // 2, body, 0)

    @pl.when(qi % 2 == 1)
    def _():
        step(qi - 1, False)

    step(qi, True)

    lam = (jnp.exp(jnp.sum(lq1[...] * lk1[...], axis=-1, keepdims=True))
           - jnp.exp(jnp.sum(lq2[...] * lk2[...], axis=-1, keepdims=True)) + lam_init)
    for r in range(per):
        a1, a2 = acc_ref[r], acc_ref[per + r]
        of = a1[:, :LANES] / a1[:, LANES:] - lam * (a2[:, :LANES] / a2[:, LANES:])
        of = of * lax.rsqrt(jnp.mean(of * of, axis=-1, keepdims=True) + RMS_EPS)
        of = of * sg_ref[...] * (1.0 - lam_init)
        o_ref[r * rows:(r + 1) * rows, :] = of.astype(o_ref.dtype)


def _diff_attention(zqk, zv, lq1, lk1, lq2, lk2, subln_g, *, batch, seq, lam_init,
                    tq=512, rows=128):
    T = zqk.shape[0]
    H = DIFF_HEADS
    nq = seq // tq
    n_groups = 2 * tq // rows
    lam_spec = pl.BlockSpec((1, DIFF_HEAD_DIM), lambda b, h, i: (0, 0))
    return pl.pallas_call(
        functools.partial(_diff_attn_kernel, tq=tq, rows=rows, lam_init=lam_init),
        grid=(batch, H, nq),
        in_specs=[pl.BlockSpec((tq, LANES), lambda b, h, i: (b * nq + i, h)),
                  pl.BlockSpec((seq, LANES), lambda b, h, i: (b, H + h)),
                  pl.BlockSpec((seq, LANES), lambda b, h, i: (b, h)),
                  lam_spec, lam_spec, lam_spec, lam_spec,
                  pl.BlockSpec((1, LANES), lambda b, h, i: (0, 0))],
        out_specs=pl.BlockSpec((tq, LANES), lambda b, h, i: (b * nq + i, h)),
        out_shape=jax.ShapeDtypeStruct((T, H * LANES), BF16),
        scratch_shapes=[pltpu.VMEM((n_groups, rows, LANES), BF16),
                        pltpu.VMEM((seq, 2 * LANES), BF16),
                        pltpu.VMEM((n_groups, rows, LANES), F32),
                        pltpu.VMEM((n_groups, rows, 2 * LANES), F32)],
        compiler_params=_params("parallel", "parallel", "arbitrary"),
        name="diff_attention",
    )(zqk, zqk, zv,
      lq1.reshape(1, -1), lk1.reshape(1, -1), lq2.reshape(1, -1), lk2.reshape(1, -1),
      subln_g.reshape(1, -1))


def _cross_attn_kernel(q_ref, k_ref, v_ref, o_ref, *, dh, scale):
    for h in range(CROSS_HEADS):
        sl = slice(h * dh, (h + 1) * dh)
        s = lax.dot_general(q_ref[:, sl], k_ref[:, sl], _NT,
                            preferred_element_type=F32) * scale
        p = jnp.exp(s - jnp.max(s, axis=-1, keepdims=True))
        l = jnp.sum(p, axis=-1, keepdims=True)
        o = jnp.dot(p.astype(BF16), v_ref[:, sl], preferred_element_type=F32)
        o_ref[:, sl] = (o / l).astype(o_ref.dtype)


def _cross_attention(q, kv, *, batch, seq, mem_len, tq=512):
    T, D = q.shape
    dh = D // CROSS_HEADS
    nq = seq // tq
    return pl.pallas_call(
        functools.partial(_cross_attn_kernel, dh=dh, scale=dh ** -0.5),
        grid=(batch, nq),
        in_specs=[pl.BlockSpec((tq, D), lambda b, i: (b * nq + i, 0)),
                  pl.BlockSpec((mem_len, D), lambda b, i: (b, 0)),
                  pl.BlockSpec((mem_len, D), lambda b, i: (b, 1))],
        out_specs=pl.BlockSpec((tq, D), lambda b, i: (b * nq + i, 0)),
        out_shape=jax.ShapeDtypeStruct((T, D), BF16),
        compiler_params=_params("parallel", "arbitrary"),
        name="cross_attention",
    )(q, kv, kv)


def _merge_exchange_pairs(n):
    t = max(1, (n - 1).bit_length())
    pairs = []
    p = 1 << (t - 1)
    while p > 0:
        q, r, d = 1 << (t - 1), 0, p
        while d > 0:
            pairs += [(i, i + d) for i in range(n - d) if (i & p) == r]
            d, q, r = q - p, q >> 1, p
        p >>= 1
    return pairs


def _top_values(rows, n):
    s = list(rows)
    for i, j in _merge_exchange_pairs(len(s)):
        s[i], s[j] = jnp.maximum(s[i], s[j]), jnp.minimum(s[i], s[j])
    ninf = jnp.full_like(s[0], -jnp.inf)
    vals = []
    for k in range(n):
        m = jnp.max(s[0], axis=0, keepdims=True)
        vals.append(m)
        hit = s[0] == m
        for r in range(min(len(s), n - k - 1)):
            s[r] = jnp.where(hit, s[r + 1] if r + 1 < len(s) else ninf, s[r])
    return vals


def _peer_route_kernel(q_ref, k1_ref, k2_ref, thr_ref, e1_ref, e2_ref):
    K = PEER_TOPK
    q = q_ref[...]
    s1 = lax.dot_general(k1_ref[0], q[:, :PEER_N_KEYS], _NT, preferred_element_type=F32)
    s2 = lax.dot_general(k2_ref[0], q[:, PEER_N_KEYS:], _NT, preferred_element_type=F32)
    S = SUBLANES
    assert K == 2 * S
    v1 = _top_values([s1[r:r + S] for r in range(0, PEER_N_KEYS, S)], K + 1)
    v2 = _top_values([s2[r:r + S] for r in range(0, PEER_N_KEYS, S)], K + 1)
    v1a, v1b = jnp.concatenate(v1[:S], axis=0), jnp.concatenate(v1[S:K], axis=0)
    v2a, v2b = jnp.concatenate(v2[:S], axis=0), jnp.concatenate(v2[S:K], axis=0)
    ninf = jnp.full_like(v1[0], -jnp.inf)
    cand = [v1[0] + v2a, v1[0] + v2b]
    cand += [v1[a] + v2a for a in range(1, S)]
    cand += [v1b + v2[0]]
    cand += [jnp.concatenate([v1[K] + v2[0], v1[0] + v2[K]] + [ninf] * (S - 2), axis=0)]
    top = _top_values(cand, K + 1)
    z = jnp.ones_like(top[0])
    for k in range(1, K):
        z = z + jnp.exp(top[k] - top[0])
    tau = 0.5 * (top[K - 1] + top[K])
    thr_ref[0] = jnp.exp((tau - v2[0]) - s1)
    e1_ref[0] = jnp.exp(s1 - v1[0]) * (1.0 / z)
    e2_ref[0] = jnp.exp(s2 - v2[0])


def _peer_route(q, keys1, keys2, *, tm=1024):
    T = q.shape[0]
    H = PEER_HEADS
    kspec = pl.BlockSpec((1, PEER_N_KEYS, PEER_N_KEYS), lambda i, h: (h, 0, 0))
    ospec = pl.BlockSpec((1, PEER_N_KEYS, tm), lambda i, h: (h, 0, i))
    oshape = jax.ShapeDtypeStruct((H, PEER_N_KEYS, T), F32)
    return pl.pallas_call(
        _peer_route_kernel,
        grid=(T // tm, H),
        in_specs=[pl.BlockSpec((tm, 2 * PEER_N_KEYS), lambda i, h: (i, h)), kspec, kspec],
        out_specs=[ospec, ospec, ospec],
        out_shape=[oshape, oshape, oshape],
        compiler_params=_params("parallel", "arbitrary"),
        name="peer_route",
    )(q, keys1, keys2)


def _peer_dense_kernel(xnt_ref, u_ref, v_ref, thr_c, thr_p, e1_c, e1_p, e2_c, e2_p,
                       res_ref, g_ref, o_ref, y_ref, ht0, ht1, wg0, wg1, wg0_prev,
                       *, tm, half, n_exp, n_blocks, n_sub):
    g = pl.program_id(0)
    d_model = y_ref.shape[1]
    kh = half // LANES
    prev = jnp.maximum(g - 1, 0)

    @pl.when(g == 0)
    def _():
        ht1[...] = jnp.zeros(ht1.shape, F32)
        wg0_prev[...] = jnp.zeros(wg0_prev.shape, BF16)

    @pl.when(prev % n_exp == 0)
    def _():
        y_ref[...] = jnp.zeros(y_ref.shape, F32)

    first_valid = (g >= 1).astype(F32)
    second_valid = (g <= n_blocks - 1).astype(F32)

    def pair_step(u_rows, ht_w, ht_r, wg_w, wg_r, thr_ref, e1_ref, e2_ref, key0, v_rows, valid):
        half_valid = 0.5 * valid
        ra, rc = half // n_sub, d_model // n_sub
        for j in range(n_sub):
            for il in range(j * ra // LANES, (j + 1) * ra // LANES):
                rs = slice(il * LANES, (il + 1) * LANES)
                for lt in range(tm // LANES):
                    cs = slice(lt * LANES, (lt + 1) * LANES)
                    gate = jnp.zeros((LANES, LANES), F32)
                    for h in range(PEER_HEADS):
                        e2v = e2_ref[h, :, cs]
                        thr = thr_ref[h, key0 + il:key0 + il + 1, cs]
                        e1 = e1_ref[h, key0 + il:key0 + il + 1, cs]
                        gate = gate + jnp.where(e2v >= thr, e2v, 0.0) * e1
                    hh = ht_r[rs, cs]
                    act = (half_valid * hh) * (1.0 + lax.erf(hh * (2.0 ** -0.5)))
                    wg_w[cs, rs] = (gate * act).T.astype(BF16)
            ht_w[j * ra:(j + 1) * ra, :] = jnp.dot(
                u_ref[u_rows + j * ra:u_rows + (j + 1) * ra, :], xnt_ref[...],
                preferred_element_type=F32)
            y_ref[:, j * rc:(j + 1) * rc] += jnp.dot(
                wg_r[...], v_ref[v_rows:v_rows + half, j * rc:(j + 1) * rc],
                preferred_element_type=F32)

    pair_step(0, ht0, ht1, wg1, wg0_prev, thr_p, e1_p, e2_p, kh, 0, first_valid)
    pair_step(half, ht1, ht0, wg0, wg1, thr_c, e1_c, e2_c, 0, half, second_valid)

    @pl.when(g < n_blocks)
    def _():
        wg0_prev[...] = wg0[...]

    @pl.when(jnp.logical_and(g >= 1, prev % n_exp == n_exp - 1))
    def _():
        h = res_ref[...] + y_ref[...]
        ms = jnp.mean(h * h, axis=-1, keepdims=True)
        o_ref[...] = h * lax.rsqrt(ms + RMS_EPS) * g_ref[...]


def _peer_dense(xnt, u, v, thr, e1, e2, res, g, *, tm=512, te=1024, n_sub=2):
    D, T = xnt.shape
    E = u.shape[0]
    H = PEER_HEADS
    kt = te // LANES
    n_exp = E // te
    n_blocks = (T // tm) * n_exp

    def cur(g):
        b = jnp.minimum(g, n_blocks - 1)
        return b // n_exp, b % n_exp

    def prev(g):
        b = jnp.maximum(g - 1, 0)
        return b // n_exp, b % n_exp

    once = dict(pipeline_mode=pl.Buffered(1))

    def key_spec(f):
        return pl.BlockSpec((H, kt, tm), lambda g: (0, f(g)[1], f(g)[0]))

    def tok_spec(f):
        return pl.BlockSpec((H, PEER_N_KEYS, tm), lambda g: (0, 0, f(g)[0]), **once)

    return pl.pallas_call(
        functools.partial(_peer_dense_kernel, tm=tm, half=te // 2, n_exp=n_exp,
                          n_blocks=n_blocks, n_sub=n_sub),
        grid=(n_blocks + 1,),
        in_specs=[pl.BlockSpec((D, tm), lambda g: (0, cur(g)[0]), **once),
                  pl.BlockSpec((te, D), lambda g: (cur(g)[1], 0)),
                  pl.BlockSpec((te, D), lambda g: (prev(g)[1], 0)),
                  key_spec(cur), key_spec(prev), key_spec(cur), key_spec(prev),
                  tok_spec(cur), tok_spec(prev),
                  pl.BlockSpec((tm, D), lambda g: (prev(g)[0], 0), **once),
                  pl.BlockSpec((1, D), lambda g: (0, 0), **once)],
        out_specs=pl.BlockSpec((tm, D), lambda g: (prev(g)[0], 0)),
        out_shape=jax.ShapeDtypeStruct((T, D), F32),
        scratch_shapes=[pltpu.VMEM((tm, D), F32),
                        pltpu.VMEM((te // 2, tm), F32), pltpu.VMEM((te // 2, tm), F32),
                        pltpu.VMEM((tm, te // 2), BF16), pltpu.VMEM((tm, te // 2), BF16),
                        pltpu.VMEM((tm, te // 2), BF16)],
        compiler_params=_params("arbitrary"),
        name="peer_dense",
    )(xnt, u, v, thr, thr, e1, e1, e2, e2, res, g.reshape(1, D))


def kernel(x, mem, norm_mix_g, w_in, conv_dw_w, conv_dw_b, conv_ln_g, conv_ln_b,
           lambda_q1, lambda_k1, lambda_q2, lambda_k2, diff_subln_g, w_out,
           norm_cross_g, norm_mem_g, w_cq, w_ckv, w_co,
           norm_peer_g, w_pq, peer_keys1, peer_keys2, peer_u, peer_v, final_norm_g):
    B, S, D = x.shape
    T = B * S
    mem_len = mem.shape[1]
    depth = w_in.shape[0]
    conv_w = conv_dw_w.shape[2]
    n_conv = 2 * conv_w
    qk_w = DIFF_HEADS * 2 * DIFF_HEAD_DIM

    assert depth == 1, "the final norm is fused into the PEER kernel of the only layer"
    h = x.reshape(T, D)
    for l in range(depth):
        lam_init = 0.8 - 0.6 * math.exp(-0.3 * l)
        q_scale = DIFF_HEAD_DIM ** -0.5 * math.log2(math.e)
        zc = _norm_matmul(h, norm_mix_g[l], w_in[l], F32, tm=512, tn=n_conv, n_cols=n_conv,
                          name="in_proj_conv")
        zqk = _norm_matmul(h, norm_mix_g[l], w_in[l], BF16, tm=512, tn=2 * qk_w, col0=n_conv,
                           n_cols=2 * qk_w, scaled_cols=qk_w, scale=q_scale, name="in_proj_qk")
        zv = _norm_matmul(h, norm_mix_g[l], w_in[l], BF16, tm=512, tn=conv_w,
                          col0=n_conv + 2 * qk_w, name="in_proj_v")
        conv_out = _conv_group(zc, conv_dw_w[l], conv_dw_b[l], conv_ln_g[l], conv_ln_b[l],
                               batch=B, seq=S)
        attn_out = _diff_attention(zqk, zv, lambda_q1[l], lambda_k1[l], lambda_q2[l], lambda_k2[l],
                                   diff_subln_g[l], batch=B, seq=S, lam_init=lam_init)
        h = _matmul_residual([(conv_out, w_out[l], 0), (attn_out, w_out[l], 1)], h,
                             tm=512, tn=D, name="out_proj")
        kv = _norm_matmul(mem.reshape(B * mem_len, D), norm_mem_g[l], w_ckv[l], BF16,
                          tm=B * mem_len, tn=512, name="cross_kv_proj")
        cq = _norm_matmul(h, norm_cross_g[l], w_cq[l], BF16, tm=512, tn=D, name="cross_q_proj")
        co = _cross_attention(cq, kv, batch=B, seq=S, mem_len=mem_len)
        h = _matmul_residual([(co, w_co[l], 0)], h, tm=512, tn=D, name="cross_out_proj")
        pq, xnt = _norm_matmul(h, norm_peer_g[l], w_pq[l], BF16, tm=512, tn=w_pq.shape[2],
                               emit_transposed=True, name="peer_q_proj")
        thr, e1, e2 = _peer_route(pq, peer_keys1[l].astype(BF16), peer_keys2[l].astype(BF16))
        h = _peer_dense(xnt, peer_u[l].astype(BF16), peer_v[l].astype(BF16),
                        thr, e1, e2, h, final_norm_g)
    return h.reshape(B, S, D)
```

```python
import functools
import math

import jax
import jax.numpy as jnp
from jax import lax
from jax.experimental import pallas as pl
from jax.experimental.pallas import tpu as pltpu

F32 = jnp.float32
BF16 = jnp.bfloat16

RMS_EPS = 1e-6
LN_EPS = 1e-5
NEG_INF = -1e30

LANES = 128
SUBLANES = 8
CONV_KERNEL = 31
CONV_HALO = 32
DIFF_HEADS = 8
DIFF_HEAD_DIM = 64
CROSS_HEADS = 4
PEER_HEADS = 8
PEER_N_KEYS = 128
PEER_TOPK = 16
VMEM_LIMIT = 56 * 1024 * 1024

_NT = (((1,), (1,)), ((), ()))


def _params(*sem):
    return pltpu.CompilerParams(dimension_semantics=sem, vmem_limit_bytes=VMEM_LIMIT)


def _norm_mm_kernel(x_ref, g_ref, w_ref, *rest, emit_transposed, scaled_cols, scale):
    o_ref, hn_ref = rest[0], rest[-1]

    @pl.when(pl.program_id(1) == 0)
    def _():
        x = x_ref[...]
        ms = jnp.mean(x * x, axis=-1, keepdims=True)
        hn = x * lax.rsqrt(ms + RMS_EPS) * g_ref[...]
        hn_ref[...] = hn.astype(BF16)
        if emit_transposed:
            rest[1][...] = hn.T.astype(BF16)

    acc = jnp.dot(hn_ref[...], w_ref[...].astype(BF16), preferred_element_type=F32)
    if scaled_cols:
        first = jnp.where(pl.program_id(1) == 0, scale, 1.0)
        o_ref[:, :scaled_cols] = (acc[:, :scaled_cols] * first).astype(o_ref.dtype)
        if scaled_cols < acc.shape[1]:
            o_ref[:, scaled_cols:] = acc[:, scaled_cols:].astype(o_ref.dtype)
    else:
        o_ref[...] = acc.astype(o_ref.dtype)


def _norm_matmul(x, g, w, out_dtype, *, tm, tn, name, col0=0, n_cols=None,
                 emit_transposed=False, scaled_cols=0, scale=1.0):
    M, K = x.shape
    n_cols = w.shape[1] - col0 if n_cols is None else n_cols
    assert col0 % tn == 0 and scaled_cols <= tn
    cb0 = col0 // tn
    w_mode = dict(pipeline_mode=pl.Buffered(1)) if n_cols == tn else {}
    out_specs = [pl.BlockSpec((tm, tn), lambda i, j: (i, j))]
    out_shape = [jax.ShapeDtypeStruct((M, n_cols), out_dtype)]
    if emit_transposed:
        out_specs.append(pl.BlockSpec((K, tm), lambda i, j: (0, i)))
        out_shape.append(jax.ShapeDtypeStruct((K, M), BF16))
    out = pl.pallas_call(
        functools.partial(_norm_mm_kernel, emit_transposed=emit_transposed,
                          scaled_cols=scaled_cols, scale=scale),
        grid=(M // tm, n_cols // tn),
        in_specs=[pl.BlockSpec((tm, K), lambda i, j: (i, 0)),
                  pl.BlockSpec((1, K), lambda i, j: (0, 0)),
                  pl.BlockSpec((K, tn), lambda i, j: (0, j + cb0), **w_mode)],
        out_specs=out_specs,
        out_shape=out_shape,
        scratch_shapes=[pltpu.VMEM((tm, K), BF16)],
        compiler_params=_params("parallel", "arbitrary"),
        name=name,
    )(x, g.reshape(1, K), w)
    return out if emit_transposed else out[0]


def _mm_res_kernel(*refs, n_pairs):
    res_ref, o_ref = refs[2 * n_pairs], refs[2 * n_pairs + 1]
    acc = res_ref[...]
    for p in range(n_pairs):
        acc = acc + jnp.dot(refs[2 * p][...], refs[2 * p + 1][...].astype(BF16),
                            preferred_element_type=F32)
    o_ref[...] = acc


def _matmul_residual(pairs, res, *, tm, tn, name):
    M, N = res.shape
    w_mode = dict(pipeline_mode=pl.Buffered(1)) if tn == N else {}
    in_specs, args = [], []
    for a, w, rb in pairs:
        k = a.shape[1]
        in_specs += [pl.BlockSpec((tm, k), lambda i, j: (i, 0)),
                     pl.BlockSpec((k, tn), lambda i, j, rb=rb: (rb, j), **w_mode)]
        args += [a, w]
    in_specs.append(pl.BlockSpec((tm, tn), lambda i, j: (i, j)))
    return pl.pallas_call(
        functools.partial(_mm_res_kernel, n_pairs=len(pairs)),
        grid=(M // tm, N // tn),
        in_specs=in_specs,
        out_specs=pl.BlockSpec((tm, tn), lambda i, j: (i, j)),
        out_shape=jax.ShapeDtypeStruct((M, N), F32),
        compiler_params=_params("parallel", "arbitrary"),
        name=name,
    )(*args, res)


def _conv_kernel(a_ref, gt_ref, ah_ref, gh_ref, w_ref, b_ref, lg_ref, lb_ref, o_ref, ubuf,
                 *, ts, rows):
    i = pl.program_id(2)
    uh = ah_ref[...] * jax.nn.sigmoid(gh_ref[...])
    ubuf[0:CONV_HALO, :] = jnp.where(i > 0, uh, 0.0)
    ubuf[CONV_HALO:, :] = a_ref[...] * jax.nn.sigmoid(gt_ref[...])
    bias = b_ref[...]
    lg = lg_ref[...]
    lb = lb_ref[...]
    first = CONV_HALO - (CONV_KERNEL - 1)
    for r in range(ts // rows):
        acc = jnp.zeros((rows, LANES), F32) + bias
        for j in range(CONV_KERNEL):
            start = r * rows + first + j
            acc = acc + w_ref[j:j + 1, :] * ubuf[start:start + rows, :]
        mu = jnp.mean(acc, axis=-1, keepdims=True)
        d = acc - mu
        var = jnp.mean(d * d, axis=-1, keepdims=True)
        y = d * lax.rsqrt(var + LN_EPS) * lg + lb
        o_ref[r * rows:(r + 1) * rows, :] = (y * jax.nn.sigmoid(y)).astype(o_ref.dtype)


def _conv_group(zc, w_dw, b_dw, ln_g, ln_b, *, batch, seq, ts=512, rows=128):
    T, C2 = zc.shape
    C = C2 // 2
    G = C // LANES
    ns = seq // ts
    hb = ts // CONV_HALO

    def main(off):
        return pl.BlockSpec((ts, LANES), lambda b, g, i: (b * ns + i, g + off))

    def halo(off):
        return pl.BlockSpec(
            (CONV_HALO, LANES),
            lambda b, g, i: (jnp.maximum((b * ns + i) * hb - 1, 0), g + off))

    vec = pl.BlockSpec((1, LANES), lambda b, g, i: (0, g))
    return pl.pallas_call(
        functools.partial(_conv_kernel, ts=ts, rows=rows),
        grid=(batch, G, ns),
        in_specs=[main(0), main(G), halo(0), halo(G),
                  pl.BlockSpec((CONV_KERNEL, LANES), lambda b, g, i: (0, g)),
                  vec, vec, vec],
        out_specs=pl.BlockSpec((ts, LANES), lambda b, g, i: (b * ns + i, g)),
        out_shape=jax.ShapeDtypeStruct((T, C), BF16),
        scratch_shapes=[pltpu.VMEM((ts + CONV_HALO, LANES), F32)],
        compiler_params=_params("parallel", "parallel", "arbitrary"),
        name="conv_group",
    )(zc, zc, zc, zc, w_dw, b_dw.reshape(1, C), ln_g.reshape(1, C), ln_b.reshape(1, C))


def _diff_attn_kernel(q_ref, k_ref, v_ref, lq1, lk1, lq2, lk2, sg_ref, o_ref,
                      qs_ref, vx_ref, m_ref, acc_ref, *, tq, rows, lam_init):
    qi = pl.program_id(2)
    per = tq // rows
    n_groups = 2 * per

    @pl.when(qi == 0)
    def _():
        vx_ref[:, :LANES] = v_ref[...]
        vx_ref[:, LANES:] = jnp.ones((vx_ref.shape[0], LANES), BF16)

    q = q_ref[...]
    lane = lax.broadcasted_iota(jnp.int32, (tq, LANES), 1)
    zero = jnp.zeros_like(q)
    for c, qc in enumerate((jnp.where(lane < DIFF_HEAD_DIM, q, zero),
                            jnp.where(lane >= DIFF_HEAD_DIM, q, zero))):
        for r in range(per):
            qs_ref[c * per + r] = qc[r * rows:(r + 1) * rows]
    m_ref[...] = jnp.full(m_ref.shape, NEG_INF, F32)
    acc_ref[...] = jnp.zeros(acc_ref.shape, F32)

    def step(ki, masked):
        start = pl.multiple_of(ki * tq, tq)
        kb = k_ref[pl.ds(start, tq), :]
        vb = vx_ref[pl.ds(start, tq), :]
        for g in range(n_groups):
            s = lax.dot_general(qs_ref[g], kb, _NT, preferred_element_type=F32)
            if masked:
                row = lax.broadcasted_iota(jnp.int32, (rows, tq), 0) + (g % per) * rows
                col = lax.broadcasted_iota(jnp.int32, (rows, tq), 1)
                s = jnp.where(col <= row, s, NEG_INF)
            m_old = m_ref[g]
            m_new = jnp.maximum(m_old, jnp.max(s, axis=-1, keepdims=True))
            alpha = jnp.exp2(m_old - m_new)
            p = jnp.exp2(s - jnp.tile(m_new, (1, tq // LANES)))
            acc_ref[g] = (jnp.tile(alpha, (1, 2)) * acc_ref[g]
                          + jnp.dot(p.astype(BF16), vb, preferred_element_type=F32))
            m_ref[g] = m_new

    def body(k4, c):
        for r in range(4):
            step(4 * k4 + r, False)
        return c

    lax.fori_loop(0, qi // 4, body, 0)
    rest = qi % 4

    @pl.when(rest >= 2)
    def _():
        step(qi - rest, False)
        step(qi - rest + 1, False)

    @pl.when(rest % 2 == 1)
    def _():
        step(qi - 1, False)

    step(qi, True)

    lam = (jnp.exp(jnp.sum(lq1[...] * lk1[...], axis=-1, keepdims=True))
           - jnp.exp(jnp.sum(lq2[...] * lk2[...], axis=-1, keepdims=True)) + lam_init)
    for r in range(per):
        a1, a2 = acc_ref[r], acc_ref[per + r]
        of = a1[:, :LANES] / a1[:, LANES:] - lam * (a2[:, :LANES] / a2[:, LANES:])
        of = of * lax.rsqrt(jnp.mean(of * of, axis=-1, keepdims=True) + RMS_EPS)
        of = of * sg_ref[...] * (1.0 - lam_init)
        o_ref[r * rows:(r + 1) * rows, :] = of.astype(o_ref.dtype)


def _diff_attention(zqk, zv, lq1, lk1, lq2, lk2, subln_g, *, batch, seq, lam_init,
                    tq=512, rows=128):
    T = zqk.shape[0]
    H = DIFF_HEADS
    nq = seq // tq
    n_groups = 2 * tq // rows
    lam_spec = pl.BlockSpec((1, DIFF_HEAD_DIM), lambda b, h, i: (0, 0))
    return pl.pallas_call(
        functools.partial(_diff_attn_kernel, tq=tq, rows=rows, lam_init=lam_init),
        grid=(batch, H, nq),
        in_specs=[pl.BlockSpec((tq, LANES), lambda b, h, i: (b * nq + i, h)),
                  pl.BlockSpec((seq, LANES), lambda b, h, i: (b, H + h)),
                  pl.BlockSpec((seq, LANES), lambda b, h, i: (b, h)),
                  lam_spec, lam_spec, lam_spec, lam_spec,
                  pl.BlockSpec((1, LANES), lambda b, h, i: (0, 0))],
        out_specs=pl.BlockSpec((tq, LANES), lambda b, h, i: (b * nq + i, h)),
        out_shape=jax.ShapeDtypeStruct((T, H * LANES), BF16),
        scratch_shapes=[pltpu.VMEM((n_groups, rows, LANES), BF16),
                        pltpu.VMEM((seq, 2 * LANES), BF16),
                        pltpu.VMEM((n_groups, rows, LANES), F32),
                        pltpu.VMEM((n_groups, rows, 2 * LANES), F32)],
        compiler_params=_params("parallel", "parallel", "arbitrary"),
        name="diff_attention",
    )(zqk, zqk, zv,
      lq1.reshape(1, -1), lk1.reshape(1, -1), lq2.reshape(1, -1), lk2.reshape(1, -1),
      subln_g.reshape(1, -1))


def _cross_attn_kernel(q_ref, k_ref, v_ref, o_ref, *, dh, scale):
    for h in range(CROSS_HEADS):
        sl = slice(h * dh, (h + 1) * dh)
        s = lax.dot_general(q_ref[:, sl], k_ref[:, sl], _NT,
                            preferred_element_type=F32) * scale
        p = jnp.exp(s - jnp.max(s, axis=-1, keepdims=True))
        l = jnp.sum(p, axis=-1, keepdims=True)
        o = jnp.dot(p.astype(BF16), v_ref[:, sl], preferred_element_type=F32)
        o_ref[:, sl] = (o / l).astype(o_ref.dtype)


def _cross_attention(q, kv, *, batch, seq, mem_len, tq=512):
    T, D = q.shape
    dh = D // CROSS_HEADS
    nq = seq // tq
    return pl.pallas_call(
        functools.partial(_cross_attn_kernel, dh=dh, scale=dh ** -0.5),
        grid=(batch, nq),
        in_specs=[pl.BlockSpec((tq, D), lambda b, i: (b * nq + i, 0)),
                  pl.BlockSpec((mem_len, D), lambda b, i: (b, 0)),
                  pl.BlockSpec((mem_len, D), lambda b, i: (b, 1))],
        out_specs=pl.BlockSpec((tq, D), lambda b, i: (b * nq + i, 0)),
        out_shape=jax.ShapeDtypeStruct((T, D), BF16),
        compiler_params=_params("parallel", "arbitrary"),
        name="cross_attention",
    )(q, kv, kv)


def _merge_exchange_pairs(n):
    t = max(1, (n - 1).bit_length())
    pairs = []
    p = 1 << (t - 1)
    while p > 0:
        q, r, d = 1 << (t - 1), 0, p
        while d > 0:
            pairs += [(i, i + d) for i in range(n - d) if (i & p) == r]
            d, q, r = q - p, q >> 1, p
        p >>= 1
    return pairs


def _top_values(rows, n):
    s = list(rows)
    for i, j in _merge_exchange_pairs(len(s)):
        s[i], s[j] = jnp.maximum(s[i], s[j]), jnp.minimum(s[i], s[j])
    ninf = jnp.full_like(s[0], -jnp.inf)
    vals = []
    for k in range(n):
        m = jnp.max(s[0], axis=0, keepdims=True)
        vals.append(m)
        hit = s[0] == m
        for r in range(min(len(s), n - k - 1)):
            s[r] = jnp.where(hit, s[r + 1] if r + 1 < len(s) else ninf, s[r])
    return vals


def _peer_route_kernel(q_ref, k1_ref, k2_ref, thr_ref, e1_ref, e2_ref):
    K = PEER_TOPK
    q = q_ref[...]
    s1 = lax.dot_general(k1_ref[0], q[:, :PEER_N_KEYS], _NT, preferred_element_type=F32)
    s2 = lax.dot_general(k2_ref[0], q[:, PEER_N_KEYS:], _NT, preferred_element_type=F32)
    S = SUBLANES
    assert K == 2 * S
    v1 = _top_values([s1[r:r + S] for r in range(0, PEER_N_KEYS, S)], K + 1)
    v2 = _top_values([s2[r:r + S] for r in range(0, PEER_N_KEYS, S)], K + 1)
    v1a, v1b = jnp.concatenate(v1[:S], axis=0), jnp.concatenate(v1[S:K], axis=0)
    v2a, v2b = jnp.concatenate(v2[:S], axis=0), jnp.concatenate(v2[S:K], axis=0)
    ninf = jnp.full_like(v1[0], -jnp.inf)
    cand = [v1[0] + v2a, v1[0] + v2b]
    cand += [v1[a] + v2a for a in range(1, S)]
    cand += [v1b + v2[0]]
    cand += [jnp.concatenate([v1[K] + v2[0], v1[0] + v2[K]] + [ninf] * (S - 2), axis=0)]
    top = _top_values(cand, K + 1)
    z = jnp.ones_like(top[0])
    for k in range(1, K):
        z = z + jnp.exp(top[k] - top[0])
    tau = 0.5 * (top[K - 1] + top[K])
    thr_ref[0] = jnp.exp((tau - v2[0]) - s1)
    e1_ref[0] = jnp.exp(s1 - v1[0]) * (1.0 / z)
    e2_ref[0] = jnp.exp(s2 - v2[0])


def _peer_route(q, keys1, keys2, *, tm=1024):
    T = q.shape[0]
    H = PEER_HEADS
    kspec = pl.BlockSpec((1, PEER_N_KEYS, PEER_N_KEYS), lambda i, h: (h, 0, 0))
    ospec = pl.BlockSpec((1, PEER_N_KEYS, tm), lambda i, h: (h, 0, i))
    oshape = jax.ShapeDtypeStruct((H, PEER_N_KEYS, T), F32)
    return pl.pallas_call(
        _peer_route_kernel,
        grid=(T // tm, H),
        in_specs=[pl.BlockSpec((tm, 2 * PEER_N_KEYS), lambda i, h: (i, h)), kspec, kspec],
        out_specs=[ospec, ospec, ospec],
        out_shape=[oshape, oshape, oshape],
        compiler_params=_params("parallel", "arbitrary"),
        name="peer_route",
    )(q, keys1, keys2)


def _peer_dense_kernel(xnt_ref, u_ref, v_ref, thr_c, thr_p, e1_c, e1_p, e2_c, e2_p,
                       res_ref, g_ref, o_ref, y_ref, ht0, ht1, wg0, wg1, wg0_prev,
                       *, tm, half, n_exp, n_blocks, n_sub):
    g = pl.program_id(0)
    d_model = y_ref.shape[1]
    kh = half // LANES
    prev = jnp.maximum(g - 1, 0)

    @pl.when(g == 0)
    def _():
        ht1[...] = jnp.zeros(ht1.shape, F32)
        wg0_prev[...] = jnp.zeros(wg0_prev.shape, BF16)

    @pl.when(prev % n_exp == 0)
    def _():
        y_ref[...] = jnp.zeros(y_ref.shape, F32)

    first_valid = (g >= 1).astype(F32)
    second_valid = (g <= n_blocks - 1).astype(F32)

    def pair_step(u_rows, ht_w, ht_r, wg_w, wg_r, thr_ref, e1_ref, e2_ref, key0, v_rows, valid):
        half_valid = 0.5 * valid
        ra, rc = half // n_sub, d_model // n_sub
        for j in range(n_sub):
            for il in range(j * ra // LANES, (j + 1) * ra // LANES):
                rs = slice(il * LANES, (il + 1) * LANES)
                for lt in range(tm // LANES):
                    cs = slice(lt * LANES, (lt + 1) * LANES)
                    gate = jnp.zeros((LANES, LANES), F32)
                    for h in range(PEER_HEADS):
                        e2v = e2_ref[h, :, cs]
                        thr = thr_ref[h, key0 + il:key0 + il + 1, cs]
                        e1 = e1_ref[h, key0 + il:key0 + il + 1, cs]
                        gate = gate + jnp.where(e2v >= thr, e2v, 0.0) * e1
                    hh = ht_r[rs, cs]
                    act = (half_valid * hh) * (1.0 + lax.erf(hh * (2.0 ** -0.5)))
                    wg_w[cs, rs] = (gate * act).T.astype(BF16)
            ht_w[j * ra:(j + 1) * ra, :] = jnp.dot(
                u_ref[u_rows + j * ra:u_rows + (j + 1) * ra, :], xnt_ref[...],
                preferred_element_type=F32)
            y_ref[:, j * rc:(j + 1) * rc] += jnp.dot(
                wg_r[...], v_ref[v_rows:v_rows + half, j * rc:(j + 1) * rc],
                preferred_element_type=F32)

    pair_step(0, ht0, ht1, wg1, wg0_prev, thr_p, e1_p, e2_p, kh, 0, first_valid)
    pair_step(half, ht1, ht0, wg0, wg1, thr_c, e1_c, e2_c, 0, half, second_valid)

    @pl.when(g < n_blocks)
    def _():
        wg0_prev[...] = wg0[...]

    @pl.when(jnp.logical_and(g >= 1, prev % n_exp == n_exp - 1))
    def _():
        h = res_ref[...] + y_ref[...]
        ms = jnp.mean(h * h, axis=-1, keepdims=True)
        o_ref[...] = h * lax.rsqrt(ms + RMS_EPS) * g_ref[...]


def _peer_dense(xnt, u, v, thr, e1, e2, res, g, *, tm=512, te=1024, n_sub=2):
    D, T = xnt.shape
    E = u.shape[0]
    H = PEER_HEADS
    kt = te // LANES
    n_exp = E // te
    n_blocks = (T // tm) * n_exp

    def cur(g):
        b = jnp.minimum(g, n_blocks - 1)
        return b // n_exp, b % n_exp

    def prev(g):
        b = jnp.maximum(g - 1, 0)
        return b // n_exp, b % n_exp

    once = dict(pipeline_mode=pl.Buffered(1))

    def key_spec(f):
        return pl.BlockSpec((H, kt, tm), lambda g: (0, f(g)[1], f(g)[0]))

    def tok_spec(f):
        return pl.BlockSpec((H, PEER_N_KEYS, tm), lambda g: (0, 0, f(g)[0]), **once)

    return pl.pallas_call(
        functools.partial(_peer_dense_kernel, tm=tm, half=te // 2, n_exp=n_exp,
                          n_blocks=n_blocks, n_sub=n_sub),
        grid=(n_blocks + 1,),
        in_specs=[pl.BlockSpec((D, tm), lambda g: (0, cur(g)[0]), **once),
                  pl.BlockSpec((te, D), lambda g: (cur(g)[1], 0)),
                  pl.BlockSpec((te, D), lambda g: (prev(g)[1], 0)),
                  key_spec(cur), key_spec(prev), key_spec(cur), key_spec(prev),
                  tok_spec(cur), tok_spec(prev),
                  pl.BlockSpec((tm, D), lambda g: (prev(g)[0], 0), **once),
                  pl.BlockSpec((1, D), lambda g: (0, 0), **once)],
        out_specs=pl.BlockSpec((tm, D), lambda g: (prev(g)[0], 0)),
        out_shape=jax.ShapeDtypeStruct((T, D), F32),
        scratch_shapes=[pltpu.VMEM((tm, D), F32),
                        pltpu.VMEM((te // 2, tm), F32), pltpu.VMEM((te // 2, tm), F32),
                        pltpu.VMEM((tm, te // 2), BF16), pltpu.VMEM((tm, te // 2), BF16),
                        pltpu.VMEM((tm, te // 2), BF16)],
        compiler_params=_params("arbitrary"),
        name="peer_dense",
    )(xnt, u, v, thr, thr, e1, e1, e2, e2, res, g.reshape(1, D))


def kernel(x, mem, norm_mix_g, w_in, conv_dw_w, conv_dw_b, conv_ln_g, conv_ln_b,
           lambda_q1, lambda_k1, lambda_q2, lambda_k2, diff_subln_g, w_out,
           norm_cross_g, norm_mem_g, w_cq, w_ckv, w_co,
           norm_peer_g, w_pq, peer_keys1, peer_keys2, peer_u, peer_v, final_norm_g):
    B, S, D = x.shape
    T = B * S
    mem_len = mem.shape[1]
    depth = w_in.shape[0]
    conv_w = conv_dw_w.shape[2]
    n_conv = 2 * conv_w
    qk_w = DIFF_HEADS * 2 * DIFF_HEAD_DIM

    assert depth == 1, "the final norm is fused into the PEER kernel of the only layer"
    h = x.reshape(T, D)
    for l in range(depth):
        lam_init = 0.8 - 0.6 * math.exp(-0.3 * l)
        q_scale = DIFF_HEAD_DIM ** -0.5 * math.log2(math.e)
        zc = _norm_matmul(h, norm_mix_g[l], w_in[l], F32, tm=512, tn=n_conv, n_cols=n_conv,
                          name="in_proj_conv")
        zqk = _norm_matmul(h, norm_mix_g[l], w_in[l], BF16, tm=512, tn=2 * qk_w, col0=n_conv,
                           n_cols=2 * qk_w, scaled_cols=qk_w, scale=q_scale, name="in_proj_qk")
        zv = _norm_matmul(h, norm_mix_g[l], w_in[l], BF16, tm=512, tn=conv_w,
                          col0=n_conv + 2 * qk_w, name="in_proj_v")
        conv_out = _conv_group(zc, conv_dw_w[l], conv_dw_b[l], conv_ln_g[l], conv_ln_b[l],
                               batch=B, seq=S)
        attn_out = _diff_attention(zqk, zv, lambda_q1[l], lambda_k1[l], lambda_q2[l], lambda_k2[l],
                                   diff_subln_g[l], batch=B, seq=S, lam_init=lam_init)
        h = _matmul_residual([(conv_out, w_out[l], 0), (attn_out, w_out[l], 1)], h,
                             tm=512, tn=D, name="out_proj")
        kv = _norm_matmul(mem.reshape(B * mem_len, D), norm_mem_g[l], w_ckv[l], BF16,
                          tm=B * mem_len, tn=512, name="cross_kv_proj")
        cq = _norm_matmul(h, norm_cross_g[l], w_cq[l], BF16, tm=512, tn=D, name="cross_q_proj")
        co = _cross_attention(cq, kv, batch=B, seq=S, mem_len=mem_len)
        h = _matmul_residual([(co, w_co[l], 0)], h, tm=512, tn=D, name="cross_out_proj")
        pq, xnt = _norm_matmul(h, norm_peer_g[l], w_pq[l], BF16, tm=512, tn=w_pq.shape[2],
                               emit_transposed=True, name="peer_q_proj")
        thr, e1, e2 = _peer_route(pq, peer_keys1[l].astype(BF16), peer_keys2[l].astype(BF16))
        h = _peer_dense(xnt, peer_u[l].astype(BF16), peer_v[l].astype(BF16),
                        thr, e1, e2, h, final_norm_g)
    return h.reshape(B, S, D)
```

```python
import functools
import math

import jax
import jax.numpy as jnp
from jax import lax
from jax.experimental import pallas as pl
from jax.experimental.pallas import tpu as pltpu

F32 = jnp.float32
BF16 = jnp.bfloat16

RMS_EPS = 1e-6
LN_EPS = 1e-5
NEG_INF = -1e30

LANES = 128
SUBLANES = 8
CONV_KERNEL = 31
CONV_HALO = 32
DIFF_HEADS = 8
DIFF_HEAD_DIM = 64
CROSS_HEADS = 4
PEER_HEADS = 8
PEER_N_KEYS = 128
PEER_TOPK = 16
VMEM_LIMIT = 56 * 1024 * 1024

_NT = (((1,), (1,)), ((), ()))


def _params(*sem):
    return pltpu.CompilerParams(dimension_semantics=sem, vmem_limit_bytes=VMEM_LIMIT)


def _norm_mm_kernel(x_ref, g_ref, w_ref, *rest, emit_transposed, scaled_cols, scale):
    o_ref, hn_ref = rest[0], rest[-1]

    @pl.when(pl.program_id(1) == 0)
    def _():
        x = x_ref[...]
        ms = jnp.mean(x * x, axis=-1, keepdims=True)
        hn = x * lax.rsqrt(ms + RMS_EPS) * g_ref[...]
        hn_ref[...] = hn.astype(BF16)
        if emit_transposed:
            rest[1][...] = hn.T.astype(BF16)

    acc = jnp.dot(hn_ref[...], w_ref[...].astype(BF16), preferred_element_type=F32)
    if scaled_cols:
        first = jnp.where(pl.program_id(1) == 0, scale, 1.0)
        o_ref[:, :scaled_cols] = (acc[:, :scaled_cols] * first).astype(o_ref.dtype)
        if scaled_cols < acc.shape[1]:
            o_ref[:, scaled_cols:] = acc[:, scaled_cols:].astype(o_ref.dtype)
    else:
        o_ref[...] = acc.astype(o_ref.dtype)


def _norm_matmul(x, g, w, out_dtype, *, tm, tn, name, col0=0, n_cols=None,
                 emit_transposed=False, scaled_cols=0, scale=1.0):
    M, K = x.shape
    n_cols = w.shape[1] - col0 if n_cols is None else n_cols
    assert col0 % tn == 0 and scaled_cols <= tn
    cb0 = col0 // tn
    w_mode = dict(pipeline_mode=pl.Buffered(1)) if n_cols == tn else {}
    out_specs = [pl.BlockSpec((tm, tn), lambda i, j: (i, j))]
    out_shape = [jax.ShapeDtypeStruct((M, n_cols), out_dtype)]
    if emit_transposed:
        out_specs.append(pl.BlockSpec((K, tm), lambda i, j: (0, i)))
        out_shape.append(jax.ShapeDtypeStruct((K, M), BF16))
    out = pl.pallas_call(
        functools.partial(_norm_mm_kernel, emit_transposed=emit_transposed,
                          scaled_cols=scaled_cols, scale=scale),
        grid=(M // tm, n_cols // tn),
        in_specs=[pl.BlockSpec((tm, K), lambda i, j: (i, 0)),
                  pl.BlockSpec((1, K), lambda i, j: (0, 0)),
                  pl.BlockSpec((K, tn), lambda i, j: (0, j + cb0), **w_mode)],
        out_specs=out_specs,
        out_shape=out_shape,
        scratch_shapes=[pltpu.VMEM((tm, K), BF16)],
        compiler_params=_params("parallel", "arbitrary"),
        name=name,
    )(x, g.reshape(1, K), w)
    return out if emit_transposed else out[0]


def _mm_res_kernel(*refs, n_pairs):
    res_ref, o_ref = refs[2 * n_pairs], refs[2 * n_pairs + 1]
    acc = res_ref[...]
    for p in range(n_pairs):
        acc = acc + jnp.dot(refs[2 * p][...], refs[2 * p + 1][...].astype(BF16),
                            preferred_element_type=F32)
    o_ref[...] = acc


def _matmul_residual(pairs, res, *, tm, tn, name):
    M, N = res.shape
    w_mode = dict(pipeline_mode=pl.Buffered(1)) if tn == N else {}
    in_specs, args = [], []
    for a, w, rb in pairs:
        k = a.shape[1]
        in_specs += [pl.BlockSpec((tm, k), lambda i, j: (i, 0)),
                     pl.BlockSpec((k, tn), lambda i, j, rb=rb: (rb, j), **w_mode)]
        args += [a, w]
    in_specs.append(pl.BlockSpec((tm, tn), lambda i, j: (i, j)))
    return pl.pallas_call(
        functools.partial(_mm_res_kernel, n_pairs=len(pairs)),
        grid=(M // tm, N // tn),
        in_specs=in_specs,
        out_specs=pl.BlockSpec((tm, tn), lambda i, j: (i, j)),
        out_shape=jax.ShapeDtypeStruct((M, N), F32),
        compiler_params=_params("parallel", "arbitrary"),
        name=name,
    )(*args, res)


def _conv_kernel(a_ref, gt_ref, ah_ref, gh_ref, w_ref, b_ref, lg_ref, lb_ref, o_ref, ubuf,
                 *, ts, rows):
    i = pl.program_id(2)
    uh = ah_ref[...] * jax.nn.sigmoid(gh_ref[...])
    ubuf[0:CONV_HALO, :] = jnp.where(i > 0, uh, 0.0)
    ubuf[CONV_HALO:, :] = a_ref[...] * jax.nn.sigmoid(gt_ref[...])
    bias = b_ref[...]
    lg = lg_ref[...]
    lb = lb_ref[...]
    first = CONV_HALO - (CONV_KERNEL - 1)
    for r in range(ts // rows):
        acc = jnp.zeros((rows, LANES), F32) + bias
        for j in range(CONV_KERNEL):
            start = r * rows + first + j
            acc = acc + w_ref[j:j + 1, :] * ubuf[start:start + rows, :]
        mu = jnp.mean(acc, axis=-1, keepdims=True)
        d = acc - mu
        var = jnp.mean(d * d, axis=-1, keepdims=True)
        y = d * lax.rsqrt(var + LN_EPS) * lg + lb
        o_ref[r * rows:(r + 1) * rows, :] = (y * jax.nn.sigmoid(y)).astype(o_ref.dtype)


def _conv_group(zc, w_dw, b_dw, ln_g, ln_b, *, batch, seq, ts=512, rows=128):
    T, C2 = zc.shape
    C = C2 // 2
    G = C // LANES
    ns = seq // ts
    hb = ts // CONV_HALO

    def main(off):
        return pl.BlockSpec((ts, LANES), lambda b, g, i: (b * ns + i, g + off))

    def halo(off):
        return pl.BlockSpec(
            (CONV_HALO, LANES),
            lambda b, g, i: (jnp.maximum((b * ns + i) * hb - 1, 0), g + off))

    vec = pl.BlockSpec((1, LANES), lambda b, g, i: (0, g))
    return pl.pallas_call(
        functools.partial(_conv_kernel, ts=ts, rows=rows),
        grid=(batch, G, ns),
        in_specs=[main(0), main(G), halo(0), halo(G),
                  pl.BlockSpec((CONV_KERNEL, LANES), lambda b, g, i: (0, g)),
                  vec, vec, vec],
        out_specs=pl.BlockSpec((ts, LANES), lambda b, g, i: (b * ns + i, g)),
        out_shape=jax.ShapeDtypeStruct((T, C), BF16),
        scratch_shapes=[pltpu.VMEM((ts + CONV_HALO, LANES), F32)],
        compiler_params=_params("parallel", "parallel", "arbitrary"),
        name="conv_group",
    )(zc, zc, zc, zc, w_dw, b_dw.reshape(1, C), ln_g.reshape(1, C), ln_b.reshape(1, C))


def _diff_attn_kernel(q_ref, k_ref, v_ref, lq1, lk1, lq2, lk2, sg_ref, o_ref,
                      qs_ref, vx_ref, m_ref, acc_ref, *, tq, rows, lam_init):
    qi = pl.program_id(2)
    per = tq // rows
    n_groups = 2 * per

    @pl.when(qi == 0)
    def _():
        vx_ref[:, :LANES] = v_ref[...]
        vx_ref[:, LANES:] = jnp.ones((vx_ref.shape[0], LANES), BF16)

    q = q_ref[...]
    lane = lax.broadcasted_iota(jnp.int32, (tq, LANES), 1)
    zero = jnp.zeros_like(q)
    for c, qc in enumerate((jnp.where(lane < DIFF_HEAD_DIM, q, zero),
                            jnp.where(lane >= DIFF_HEAD_DIM, q, zero))):
        for r in range(per):
            qs_ref[c * per + r] = qc[r * rows:(r + 1) * rows]
    m_ref[...] = jnp.full(m_ref.shape, NEG_INF, F32)
    acc_ref[...] = jnp.zeros(acc_ref.shape, F32)

    def step(ki, masked):
        start = pl.multiple_of(ki * tq, tq)
        kb = k_ref[pl.ds(start, tq), :]
        vb = vx_ref[pl.ds(start, tq), :]
        for g in range(n_groups):
            s = lax.dot_general(qs_ref[g], kb, _NT, preferred_element_type=F32)
            if masked:
                row = lax.broadcasted_iota(jnp.int32, (rows, tq), 0) + (g % per) * rows
                col = lax.broadcasted_iota(jnp.int32, (rows, tq), 1)
                s = jnp.where(col <= row, s, NEG_INF)
            m_old = m_ref[g]
            m_new = jnp.maximum(m_old, jnp.max(s, axis=-1, keepdims=True))
            alpha = jnp.exp2(m_old - m_new)
            p = jnp.exp2(s - jnp.tile(m_new, (1, tq // LANES)))
            acc_ref[g] = (jnp.tile(alpha, (1, 2)) * acc_ref[g]
                          + jnp.dot(p.astype(BF16), vb, preferred_element_type=F32))
            m_ref[g] = m_new

    def body(k4, c):
        for r in range(4):
            step(4 * k4 + r, False)
        return c

    lax.fori_loop(0, qi // 4, body, 0)
    rest = qi % 4

    @pl.when(rest >= 2)
    def _():
        step(qi - rest, False)
        step(qi - rest + 1, False)

    @pl.when(rest % 2 == 1)
    def _():
        step(qi - 1, False)

    step(qi, True)

    lam = (jnp.exp(jnp.sum(lq1[...] * lk1[...], axis=-1, keepdims=True))
           - jnp.exp(jnp.sum(lq2[...] * lk2[...], axis=-1, keepdims=True)) + lam_init)
    for r in range(per):
        a1, a2 = acc_ref[r], acc_ref[per + r]
        of = a1[:, :LANES] / a1[:, LANES:] - lam * (a2[:, :LANES] / a2[:, LANES:])
        of = of * lax.rsqrt(jnp.mean(of * of, axis=-1, keepdims=True) + RMS_EPS)
        of = of * sg_ref[...] * (1.0 - lam_init)
        o_ref[r * rows:(r + 1) * rows, :] = of.astype(o_ref.dtype)


def _diff_attention(zqk, zv, lq1, lk1, lq2, lk2, subln_g, *, batch, seq, lam_init,
                    tq=512, rows=128):
    T = zqk.shape[0]
    H = DIFF_HEADS
    nq = seq // tq
    n_groups = 2 * tq // rows
    lam_spec = pl.BlockSpec((1, DIFF_HEAD_DIM), lambda b, h, i: (0, 0))
    return pl.pallas_call(
        functools.partial(_diff_attn_kernel, tq=tq, rows=rows, lam_init=lam_init),
        grid=(batch, H, nq),
        in_specs=[pl.BlockSpec((tq, LANES), lambda b, h, i: (b * nq + i, h)),
                  pl.BlockSpec((seq, LANES), lambda b, h, i: (b, H + h)),
                  pl.BlockSpec((seq, LANES), lambda b, h, i: (b, h)),
                  lam_spec, lam_spec, lam_spec, lam_spec,
                  pl.BlockSpec((1, LANES), lambda b, h, i: (0, 0))],
        out_specs=pl.BlockSpec((tq, LANES), lambda b, h, i: (b * nq + i, h)),
        out_shape=jax.ShapeDtypeStruct((T, H * LANES), BF16),
        scratch_shapes=[pltpu.VMEM((n_groups, rows, LANES), BF16),
                        pltpu.VMEM((seq, 2 * LANES), BF16),
                        pltpu.VMEM((n_groups, rows, LANES), F32),
                        pltpu.VMEM((n_groups, rows, 2 * LANES), F32)],
        compiler_params=_params("parallel", "parallel", "arbitrary"),
        name="diff_attention",
    )(zqk, zqk, zv,
      lq1.reshape(1, -1), lk1.reshape(1, -1), lq2.reshape(1, -1), lk2.reshape(1, -1),
      subln_g.reshape(1, -1))


def _cross_attn_kernel(q_ref, k_ref, v_ref, o_ref, *, dh, scale):
    for h in range(CROSS_HEADS):
        sl = slice(h * dh, (h + 1) * dh)
        s = lax.dot_general(q_ref[:, sl], k_ref[:, sl], _NT,
                            preferred_element_type=F32) * scale
        p = jnp.exp(s - jnp.max(s, axis=-1, keepdims=True))
        l = jnp.sum(p, axis=-1, keepdims=True)
        o = jnp.dot(p.astype(BF16), v_ref[:, sl], preferred_element_type=F32)
        o_ref[:, sl] = (o / l).astype(o_ref.dtype)


def _cross_attention(q, kv, *, batch, seq, mem_len, tq=512):
    T, D = q.shape
    dh = D // CROSS_HEADS
    nq = seq // tq
    return pl.pallas_call(
        functools.partial(_cross_attn_kernel, dh=dh, scale=dh ** -0.5),
        grid=(batch, nq),
        in_specs=[pl.BlockSpec((tq, D), lambda b, i: (b * nq + i, 0)),
                  pl.BlockSpec((mem_len, D), lambda b, i: (b, 0)),
                  pl.BlockSpec((mem_len, D), lambda b, i: (b, 1))],
        out_specs=pl.BlockSpec((tq, D), lambda b, i: (b * nq + i, 0)),
        out_shape=jax.ShapeDtypeStruct((T, D), BF16),
        compiler_params=_params("parallel", "arbitrary"),
        name="cross_attention",
    )(q, kv, kv)


def _merge_exchange_pairs(n):
    t = max(1, (n - 1).bit_length())
    pairs = []
    p = 1 << (t - 1)
    while p > 0:
        q, r, d = 1 << (t - 1), 0, p
        while d > 0:
            pairs += [(i, i + d) for i in range(n - d) if (i & p) == r]
            d, q, r = q - p, q >> 1, p
        p >>= 1
    return pairs


def _top_values(rows, n):
    s = list(rows)
    for i, j in _merge_exchange_pairs(len(s)):
        s[i], s[j] = jnp.maximum(s[i], s[j]), jnp.minimum(s[i], s[j])
    ninf = jnp.full_like(s[0], -jnp.inf)
    vals = []
    for k in range(n):
        m = jnp.max(s[0], axis=0, keepdims=True)
        vals.append(m)
        hit = s[0] == m
        for r in range(min(len(s), n - k - 1)):
            s[r] = jnp.where(hit, s[r + 1] if r + 1 < len(s) else ninf, s[r])
    return vals


def _peer_route_kernel(q_ref, k1_ref, k2_ref, thr_ref, e1_ref, e2_ref):
    K = PEER_TOPK
    q = q_ref[...]
    s1 = lax.dot_general(k1_ref[0], q[:, :PEER_N_KEYS], _NT, preferred_element_type=F32)
    s2 = lax.dot_general(k2_ref[0], q[:, PEER_N_KEYS:], _NT, preferred_element_type=F32)
    S = SUBLANES
    assert K == 2 * S
    v1 = _top_values([s1[r:r + S] for r in range(0, PEER_N_KEYS, S)], K + 1)
    v2 = _top_values([s2[r:r + S] for r in range(0, PEER_N_KEYS, S)], K + 1)
    v1a, v1b = jnp.concatenate(v1[:S], axis=0), jnp.concatenate(v1[S:K], axis=0)
    v2a, v2b = jnp.concatenate(v2[:S], axis=0), jnp.concatenate(v2[S:K], axis=0)
    ninf = jnp.full_like(v1[0], -jnp.inf)
    cand = [v1[0] + v2a, v1[0] + v2b]
    cand += [v1[a] + v2a for a in range(1, S)]
    cand += [v1b + v2[0]]
    cand += [jnp.concatenate([v1[K] + v2[0], v1[0] + v2[K]] + [ninf] * (S - 2), axis=0)]
    top = _top_values(cand, K + 1)
    z = jnp.ones_like(top[0])
    for k in range(1, K):
        z = z + jnp.exp(top[k] - top[0])
    tau = 0.5 * (top[K - 1] + top[K])
    thr_ref[0] = jnp.exp((tau - v2[0]) - s1)
    e1_ref[0] = jnp.exp(s1 - v1[0]) * (1.0 / z)
    e2_ref[0] = jnp.exp(s2 - v2[0])


def _peer_route(q, keys1, keys2, *, tm=1024):
    T = q.shape[0]
    H = PEER_HEADS
    kspec = pl.BlockSpec((1, PEER_N_KEYS, PEER_N_KEYS), lambda i, h: (h, 0, 0))
    ospec = pl.BlockSpec((1, PEER_N_KEYS, tm), lambda i, h: (h, 0, i))
    oshape = jax.ShapeDtypeStruct((H, PEER_N_KEYS, T), F32)
    return pl.pallas_call(
        _peer_route_kernel,
        grid=(T // tm, H),
        in_specs=[pl.BlockSpec((tm, 2 * PEER_N_KEYS), lambda i, h: (i, h)), kspec, kspec],
        out_specs=[ospec, ospec, ospec],
        out_shape=[oshape, oshape, oshape],
        compiler_params=_params("parallel", "arbitrary"),
        name="peer_route",
    )(q, keys1, keys2)


def _peer_dense_kernel(xnt_ref, u_ref, v_ref, thr_c, thr_p, e1_c, e1_p, e2_c, e2_p,
                       res_ref, g_ref, o_ref, y_ref, ht0, ht1, wg0, wg1, wg0_prev,
                       *, tm, half, n_exp, n_blocks, n_sub):
    g = pl.program_id(0)
    d_model = y_ref.shape[1]
    kh = half // LANES
    prev = jnp.maximum(g - 1, 0)

    @pl.when(g == 0)
    def _():
        ht1[...] = jnp.zeros(ht1.shape, F32)
        wg0_prev[...] = jnp.zeros(wg0_prev.shape, BF16)

    @pl.when(prev % n_exp == 0)
    def _():
        y_ref[...] = jnp.zeros(y_ref.shape, F32)

    first_valid = (g >= 1).astype(F32)
    second_valid = (g <= n_blocks - 1).astype(F32)

    def pair_step(u_rows, ht_w, ht_r, wg_w, wg_r, thr_ref, e1_ref, e2_ref, key0, v_rows, valid):
        half_valid = 0.5 * valid
        ra, rc = half // n_sub, d_model // n_sub
        for j in range(n_sub):
            for il in range(j * ra // LANES, (j + 1) * ra // LANES):
                rs = slice(il * LANES, (il + 1) * LANES)
                for lt in range(tm // LANES):
                    cs = slice(lt * LANES, (lt + 1) * LANES)
                    gate = jnp.zeros((LANES, LANES), F32)
                    for h in range(PEER_HEADS):
                        e2v = e2_ref[h, :, cs]
                        thr = thr_ref[h, key0 + il:key0 + il + 1, cs]
                        e1 = e1_ref[h, key0 + il:key0 + il + 1, cs]
                        gate = gate + jnp.where(e2v >= thr, e2v, 0.0) * e1
                    hh = ht_r[rs, cs]
                    act = (half_valid * hh) * (1.0 + lax.erf(hh * (2.0 ** -0.5)))
                    wg_w[cs, rs] = (gate * act).T.astype(BF16)
            ht_w[j * ra:(j + 1) * ra, :] = jnp.dot(
                u_ref[u_rows + j * ra:u_rows + (j + 1) * ra, :], xnt_ref[...],
                preferred_element_type=F32)
            y_ref[:, j * rc:(j + 1) * rc] += jnp.dot(
                wg_r[...], v_ref[v_rows:v_rows + half, j * rc:(j + 1) * rc],
                preferred_element_type=F32)

    pair_step(0, ht0, ht1, wg1, wg0_prev, thr_p, e1_p, e2_p, kh, 0, first_valid)
    pair_step(half, ht1, ht0, wg0, wg1, thr_c, e1_c, e2_c, 0, half, second_valid)

    @pl.when(g < n_blocks)
    def _():
        wg0_prev[...] = wg0[...]

    @pl.when(jnp.logical_and(g >= 1, prev % n_exp == n_exp - 1))
    def _():
        h = res_ref[...] + y_ref[...]
        ms = jnp.mean(h * h, axis=-1, keepdims=True)
        o_ref[...] = h * lax.rsqrt(ms + RMS_EPS) * g_ref[...]


def _peer_dense(xnt, u, v, thr, e1, e2, res, g, *, tm=512, te=1024, n_sub=2):
    D, T = xnt.shape
    E = u.shape[0]
    H = PEER_HEADS
    kt = te // LANES
    n_exp = E // te
    n_blocks = (T // tm) * n_exp

    def cur(g):
        b = jnp.minimum(g, n_blocks - 1)
        return b // n_exp, b % n_exp

    def prev(g):
        b = jnp.maximum(g - 1, 0)
        return b // n_exp, b % n_exp

    once = dict(pipeline_mode=pl.Buffered(1))

    def key_spec(f):
        return pl.BlockSpec((H, kt, tm), lambda g: (0, f(g)[1], f(g)[0]))

    def tok_spec(f):
        return pl.BlockSpec((H, PEER_N_KEYS, tm), lambda g: (0, 0, f(g)[0]), **once)

    return pl.pallas_call(
        functools.partial(_peer_dense_kernel, tm=tm, half=te // 2, n_exp=n_exp,
                          n_blocks=n_blocks, n_sub=n_sub),
        grid=(n_blocks + 1,),
        in_specs=[pl.BlockSpec((D, tm), lambda g: (0, cur(g)[0]), **once),
                  pl.BlockSpec((te, D), lambda g: (cur(g)[1], 0)),
                  pl.BlockSpec((te, D), lambda g: (prev(g)[1], 0)),
                  key_spec(cur), key_spec(prev), key_spec(cur), key_spec(prev),
                  tok_spec(cur), tok_spec(prev),
                  pl.BlockSpec((tm, D), lambda g: (prev(g)[0], 0), **once),
                  pl.BlockSpec((1, D), lambda g: (0, 0), **once)],
        out_specs=pl.BlockSpec((tm, D), lambda g: (prev(g)[0], 0)),
        out_shape=jax.ShapeDtypeStruct((T, D), F32),
        scratch_shapes=[pltpu.VMEM((tm, D), F32),
                        pltpu.VMEM((te // 2, tm), F32), pltpu.VMEM((te // 2, tm), F32),
                        pltpu.VMEM((tm, te // 2), BF16), pltpu.VMEM((tm, te // 2), BF16),
                        pltpu.VMEM((tm, te // 2), BF16)],
        compiler_params=_params("arbitrary"),
        name="peer_dense",
    )(xnt, u, v, thr, thr, e1, e1, e2, e2, res, g.reshape(1, D))


def _peer_dense_plain_kernel(xnt_ref, u_ref, v_ref, thr_ref, e1_ref, e2_ref, res_ref, g_ref,
                             o_ref, y_ref, ht_ref, wg_ref, *, tm, te):
    e = pl.program_id(1)

    @pl.when(e == 0)
    def _():
        y_ref[...] = jnp.zeros(y_ref.shape, F32)

    ht_ref[...] = jnp.dot(u_ref[...], xnt_ref[...], preferred_element_type=F32)
    for il in range(te // LANES):
        rs = slice(il * LANES, (il + 1) * LANES)
        for lt in range(tm // LANES):
            cs = slice(lt * LANES, (lt + 1) * LANES)
            gate = jnp.zeros((LANES, LANES), F32)
            for h in range(PEER_HEADS):
                e2v = e2_ref[h, :, cs]
                gate = gate + jnp.where(e2v >= thr_ref[h, il:il + 1, cs], e2v, 0.0) \
                    * e1_ref[h, il:il + 1, cs]
            hh = ht_ref[rs, cs]
            act = (0.5 * hh) * (1.0 + lax.erf(hh * (2.0 ** -0.5)))
            wg_ref[cs, rs] = (gate * act).T.astype(BF16)
    y_ref[...] += jnp.dot(wg_ref[...], v_ref[...], preferred_element_type=F32)

    @pl.when(e == pl.num_programs(1) - 1)
    def _():
        h = res_ref[...] + y_ref[...]
        ms = jnp.mean(h * h, axis=-1, keepdims=True)
        o_ref[...] = h * lax.rsqrt(ms + RMS_EPS) * g_ref[...]


def _peer_dense_plain(xnt, u, v, thr, e1, e2, res, g, *, tm=512, te=1024):
    D, T = xnt.shape
    E = u.shape[0]
    H = PEER_HEADS
    once = dict(pipeline_mode=pl.Buffered(1))
    key_spec = pl.BlockSpec((H, te // LANES, tm), lambda i, e: (0, e, i))
    return pl.pallas_call(
        functools.partial(_peer_dense_plain_kernel, tm=tm, te=te),
        grid=(T // tm, E // te),
        in_specs=[pl.BlockSpec((D, tm), lambda i, e: (0, i), **once),
                  pl.BlockSpec((te, D), lambda i, e: (e, 0)),
                  pl.BlockSpec((te, D), lambda i, e: (e, 0)),
                  key_spec, key_spec,
                  pl.BlockSpec((H, PEER_N_KEYS, tm), lambda i, e: (0, 0, i), **once),
                  pl.BlockSpec((tm, D), lambda i, e: (i, 0), **once),
                  pl.BlockSpec((1, D), lambda i, e: (0, 0), **once)],
        out_specs=pl.BlockSpec((tm, D), lambda i, e: (i, 0)),
        out_shape=jax.ShapeDtypeStruct((T, D), F32),
        scratch_shapes=[pltpu.VMEM((tm, D), F32),
                        pltpu.VMEM((te, tm), F32),
                        pltpu.VMEM((tm, te), BF16)],
        compiler_params=_params("parallel", "arbitrary"),
        name="peer_dense",
    )(xnt, u, v, thr, e1, e2, res, g.reshape(1, D))


def kernel(x, mem, norm_mix_g, w_in, conv_dw_w, conv_dw_b, conv_ln_g, conv_ln_b,
           lambda_q1, lambda_k1, lambda_q2, lambda_k2, diff_subln_g, w_out,
           norm_cross_g, norm_mem_g, w_cq, w_ckv, w_co,
           norm_peer_g, w_pq, peer_keys1, peer_keys2, peer_u, peer_v, final_norm_g):
    B, S, D = x.shape
    T = B * S
    mem_len = mem.shape[1]
    depth = w_in.shape[0]
    conv_w = conv_dw_w.shape[2]
    n_conv = 2 * conv_w
    qk_w = DIFF_HEADS * 2 * DIFF_HEAD_DIM

    assert depth == 1, "the final norm is fused into the PEER kernel of the only layer"
    h = x.reshape(T, D)
    for l in range(depth):
        lam_init = 0.8 - 0.6 * math.exp(-0.3 * l)
        q_scale = DIFF_HEAD_DIM ** -0.5 * math.log2(math.e)
        zc = _norm_matmul(h, norm_mix_g[l], w_in[l], F32, tm=512, tn=n_conv, n_cols=n_conv,
                          name="in_proj_conv")
        zqk = _norm_matmul(h, norm_mix_g[l], w_in[l], BF16, tm=512, tn=2 * qk_w, col0=n_conv,
                           n_cols=2 * qk_w, scaled_cols=qk_w, scale=q_scale, name="in_proj_qk")
        zv = _norm_matmul(h, norm_mix_g[l], w_in[l], BF16, tm=512, tn=conv_w,
                          col0=n_conv + 2 * qk_w, name="in_proj_v")
        conv_out = _conv_group(zc, conv_dw_w[l], conv_dw_b[l], conv_ln_g[l], conv_ln_b[l],
                               batch=B, seq=S)
        attn_out = _diff_attention(zqk, zv, lambda_q1[l], lambda_k1[l], lambda_q2[l], lambda_k2[l],
                                   diff_subln_g[l], batch=B, seq=S, lam_init=lam_init)
        h = _matmul_residual([(conv_out, w_out[l], 0), (attn_out, w_out[l], 1)], h,
                             tm=512, tn=D, name="out_proj")
        kv = _norm_matmul(mem.reshape(B * mem_len, D), norm_mem_g[l], w_ckv[l], BF16,
                          tm=B * mem_len, tn=512, name="cross_kv_proj")
        cq = _norm_matmul(h, norm_cross_g[l], w_cq[l], BF16, tm=512, tn=D, name="cross_q_proj")
        co = _cross_attention(cq, kv, batch=B, seq=S, mem_len=mem_len)
        h = _matmul_residual([(co, w_co[l], 0)], h, tm=512, tn=D, name="cross_out_proj")
        pq, xnt = _norm_matmul(h, norm_peer_g[l], w_pq[l], BF16, tm=512, tn=w_pq.shape[2],
                               emit_transposed=True, name="peer_q_proj")
        thr, e1, e2 = _peer_route(pq, peer_keys1[l].astype(BF16), peer_keys2[l].astype(BF16))
        h = _peer_dense_plain(xnt, peer_u[l].astype(BF16), peer_v[l].astype(BF16),
                              thr, e1, e2, h, final_norm_g)
    return h.reshape(B, S, D)
```
